```python
import math
import jax, jax.numpy as jnp
from jax import lax
import numpy as np

D_MODEL = 1024
BATCH = 8
SEQ = 8192
DEPTH = 2
DEC_BATCH = 8
DEC_SEQ = 4096
PAST_LEN = 128

ATT_GROUPS = ((128, 1), (512, 4), (2048, 16))
ATT_HEADS = 8
ATT_HEAD_DIM = 64
ATT_GROUP_WIDTH = ATT_HEADS * ATT_HEAD_DIM
ATT_QKV_WIDTH = len(ATT_GROUPS) * 3 * ATT_GROUP_WIDTH
ROPE_DIM = ATT_HEAD_DIM // 4
ROPE_THETA = 500000.0
MLSTM_HEADS = 4
MLSTM_QK_DIM = D_MODEL // (2 * MLSTM_HEADS)
MLSTM_V_DIM = D_MODEL // MLSTM_HEADS
MLSTM_CHUNK = 128
MLSTM_IN_WIDTH = 2 * MLSTM_HEADS * MLSTM_QK_DIM + 2 * MLSTM_HEADS * MLSTM_V_DIM + 4 * MLSTM_HEADS
FFN_HIDDEN = int(math.ceil(8 * D_MODEL / 3 / 256)) * 256
N_MIXERS = 2
RMS_EPS = 1e-6

kernel_name = "hybrid_dilated_attn_mlstm_encoder"


def rms_norm(x, g):
    xf = x.astype(jnp.float32)
    y = xf * lax.rsqrt(jnp.mean(xf * xf, axis=-1, keepdims=True) + RMS_EPS)
    return (y * g.astype(jnp.float32)).astype(x.dtype)


def rope_partial(x, pos):
    half = ROPE_DIM // 2
    inv = jnp.float32(ROPE_THETA) ** (-jnp.arange(half, dtype=jnp.float32) / half)
    ang = pos.astype(jnp.float32)[:, None] * inv[None, :]
    cos = jnp.cos(ang)[None, :, None, :]
    sin = jnp.sin(ang)[None, :, None, :]
    xr = x[..., :ROPE_DIM].astype(jnp.float32)
    x1, x2 = xr[..., :half], xr[..., half:]
    rot = jnp.concatenate([x1 * cos - x2 * sin, x2 * cos + x1 * sin], axis=-1).astype(x.dtype)
    return jnp.concatenate([rot, x[..., ROPE_DIM:]], axis=-1)


def dilated_window_attention(q, k, v, window, dilation):
    B, S, H, Dh = q.shape
    r = window // (2 * dilation)
    L = S // dilation
    nb = -(-L // r)
    Lp = nb * r

    def phases(t):
        t = t.reshape(B, L, dilation, H, Dh).transpose(0, 2, 1, 3, 4)
        return jnp.pad(t, ((0, 0), (0, 0), (0, Lp - L), (0, 0), (0, 0)))

    qb = phases(q).reshape(B, dilation, nb, r, H, Dh)
    pad = ((0, 0), (0, 0), (r, r), (0, 0), (0, 0))
    kp = jnp.pad(phases(k), pad)
    vp = jnp.pad(phases(v), pad)
    kb = jnp.concatenate([kp[:, :, j * r:j * r + Lp].reshape(B, dilation, nb, r, H, Dh) for j in range(3)], axis=3)
    vb = jnp.concatenate([vp[:, :, j * r:j * r + Lp].reshape(B, dilation, nb, r, H, Dh) for j in range(3)], axis=3)

    blk = jnp.arange(nb)[:, None, None]
    qi = jnp.arange(r)[None, :, None]
    kt = jnp.arange(3 * r)[None, None, :]
    kpos = blk * r - r + kt
    rel = kt - r - qi
    mask = (jnp.abs(rel) <= r) & (kpos >= 0) & (kpos < L)

    s = jnp.einsum('bpnqhd,bpnkhd->bpnhqk', qb, kb).astype(jnp.float32) * (Dh ** -0.5)
    s = jnp.where(mask[None, None, :, None], s, -jnp.inf)
    m = jnp.max(s, axis=-1, keepdims=True)
    p = jnp.exp(s - m)
    den = jnp.sum(p, axis=-1)
    o = jnp.einsum('bpnhqk,bpnkhd->bpnqhd', p, vb.astype(jnp.float32))
    o = o / jnp.transpose(den, (0, 1, 2, 4, 3))[..., None]
    lse = m[..., 0] + jnp.log(den)
    o = o.reshape(B, dilation, Lp, H, Dh)[:, :, :L].transpose(0, 2, 1, 3, 4).reshape(B, S, H, Dh)
    lse = jnp.transpose(lse, (0, 1, 2, 4, 3)).reshape(B, dilation, Lp, H)[:, :, :L]
    lse = lse.transpose(0, 2, 1, 3).reshape(B, S, H)
    return o, lse


def dilated_attention_mixer(h, w_qkv, w_o):
    B, S, _ = h.shape
    proj = (h @ w_qkv).reshape(B, S, len(ATT_GROUPS), 3, ATT_HEADS, ATT_HEAD_DIM)
    pos = jnp.arange(S)
    outs, lses = [], []
    for g, (window, dil) in enumerate(ATT_GROUPS):
        q = rope_partial(proj[:, :, g, 0], pos)
        k = rope_partial(proj[:, :, g, 1], pos)
        v = proj[:, :, g, 2]
        o, lse = dilated_window_attention(q, k, v, window, dil)
        outs.append(o)
        lses.append(lse)
    wts = jax.nn.softmax(jnp.stack(lses, axis=0), axis=0)
    y = jnp.sum(wts[..., None] * jnp.stack(outs, axis=0), axis=0).astype(h.dtype)
    return y.reshape(B, S, ATT_GROUP_WIDTH) @ w_o


def mlstm_scan(q, k, v, i_pre, logf):
    B, S, H, dk = q.shape
    dv = v.shape[-1]
    L = MLSTM_CHUNK
    nc = S // L

    def vec_chunks(t):
        return t.reshape(B, nc, L, H, t.shape[-1]).transpose(1, 0, 3, 2, 4)

    def gate_chunks(t):
        return t.reshape(B, nc, L, H).transpose(1, 0, 3, 2)

    tril = jnp.tril(jnp.ones((L, L), dtype=bool))

    def body(carry, xs):
        C, n, m = carry
        qc, kc, vc, ic, fc = xs
        b = jnp.cumsum(fc, axis=-1)
        D = b[..., :, None] - b[..., None, :] + ic[..., None, :]
        D = jnp.where(tril, D, -jnp.inf)
        m_inter = b + m[..., None]
        m_t = jnp.maximum(m_inter, jnp.max(D, axis=-1))
        wts = jnp.exp(D - m_t[..., None])
        inter = jnp.exp(m_inter - m_t)
        sqk = jnp.einsum('bhtd,bhsd->bhts', qc, kc) * wts
        num = jnp.einsum('bhts,bhsv->bhtv', sqk, vc) + inter[..., None] * jnp.einsum('bhtd,bhdv->bhtv', qc, C)
        den = jnp.sum(sqk, axis=-1) + inter * jnp.einsum('bhtd,bhd->bht', qc, n)
        hc = num / jnp.maximum(jnp.abs(den), jnp.exp(-m_t))[..., None]
        g = b[..., -1:] - b + ic
        m_new = jnp.maximum(b[..., -1] + m, jnp.max(g, axis=-1))
        ws = jnp.exp(g - m_new[..., None])
        decay = jnp.exp(b[..., -1] + m - m_new)
        C_new = decay[..., None, None] * C + jnp.einsum('bhs,bhsd,bhsv->bhdv', ws, kc, vc)
        n_new = decay[..., None] * n + jnp.einsum('bhs,bhsd->bhd', ws, kc)
        return (C_new, n_new, m_new), hc

    init = (jnp.zeros((B, H, dk, dv), jnp.float32), jnp.zeros((B, H, dk), jnp.float32),
            jnp.zeros((B, H), jnp.float32))
    xs = (vec_chunks(q), vec_chunks(k), vec_chunks(v), gate_chunks(i_pre), gate_chunks(logf))
    _, hs = lax.scan(body, init, xs)
    return hs.transpose(1, 0, 3, 2, 4).reshape(B, S, H, dv)


def mlstm_mixer(h, w_in, b_gates, head_norm, w_out):
    B, S, _ = h.shape
    H, dk, dv = MLSTM_HEADS, MLSTM_QK_DIM, MLSTM_V_DIM
    proj = h @ w_in
    cuts = np.cumsum([H * dk, H * dk, H * dv, H * dv]).tolist()
    q, k, v, o, gates = jnp.split(proj, cuts, axis=-1)
    q = q.reshape(B, S, H, dk).astype(jnp.float32)
    k = k.reshape(B, S, H, dk).astype(jnp.float32) * (dk ** -0.5)
    v = v.reshape(B, S, H, dv).astype(jnp.float32)
    gates = (gates.astype(jnp.float32) + b_gates.astype(jnp.float32)).reshape(B, S, 4, H)
    i_fw, logf_fw = gates[:, :, 0], jax.nn.log_sigmoid(gates[:, :, 1])
    i_bw, logf_bw = gates[:, :, 2], jax.nn.log_sigmoid(gates[:, :, 3])
    flip = lambda t: jnp.flip(t, axis=1)
    h_fw = mlstm_scan(q, k, v, i_fw, logf_fw)
    h_bw = flip(mlstm_scan(flip(q), flip(k), flip(v), flip(i_bw), flip(logf_bw)))
    hs = h_fw + h_bw
    hs = hs * lax.rsqrt(jnp.mean(hs * hs, axis=-1, keepdims=True) + RMS_EPS)
    hs = hs * head_norm.astype(jnp.float32).reshape(H, dv)
    y = jax.nn.sigmoid(o) * hs.reshape(B, S, H * dv).astype(h.dtype)
    return y @ w_out


def swiglu(h, w_gu, w_down):
    a, b = jnp.split(h @ w_gu, 2, axis=-1)
    return (jax.nn.silu(a) * b) @ w_down


def run_trunk(x, c, layers, mixers, final_norm):
    for i in range(DEPTH):
        ada_w, ada_b, g1, g2, w_gu, w_down = layers[i]
        mixer = mixers[i % N_MIXERS]
        mod = (jax.nn.silu(c) @ ada_w + ada_b)[:, None, :]
        sh1, sc1, gt1, sh2, sc2, gt2 = jnp.split(mod, 6, axis=-1)
        hmix = rms_norm(x, g1) * (1 + sc1) + sh1
        x = x + gt1 * mixer(hmix)
        hffn = rms_norm(x, g2) * (1 + sc2) + sh2
        x = x + gt2 * swiglu(hffn, w_gu, w_down)
    return rms_norm(x, final_norm)


def setup_inputs(seed: int = 0) -> dict:
    key = jax.random.key(seed)
    ks = iter(jax.random.split(key, 40))
    nrm = lambda shape, s=1.0: s * jax.random.normal(next(ks), shape, jnp.float32)
    dense = lambda fi, fo, s=1.0: nrm((fi, fo), s * fi ** -0.5)
    gain = lambda n: 1.0 + nrm((n,), 0.02)
    H = MLSTM_HEADS
    b_gates = jnp.concatenate([nrm((H,), 0.1), 3.0 + nrm((H,), 0.5), nrm((H,), 0.1), 3.0 + nrm((H,), 0.5)])
    return {
        "x_prompt": nrm((BATCH, SEQ, D_MODEL)),
        "x_sample": nrm((DEC_BATCH, DEC_SEQ, D_MODEL)),
        "c_prompt": nrm((BATCH, D_MODEL)),
        "c_sample": nrm((DEC_BATCH, D_MODEL)),
        "l0_ada_w": dense(D_MODEL, 6 * D_MODEL, 0.5),
        "l0_ada_b": nrm((6 * D_MODEL,), 0.02),
        "l0_norm1": gain(D_MODEL),
        "l0_attn_w_qkv": dense(D_MODEL, ATT_QKV_WIDTH),
        "l0_attn_w_o": dense(ATT_GROUP_WIDTH, D_MODEL),
        "l0_norm2": gain(D_MODEL),
        "l0_ffn_w_gu": dense(D_MODEL, 2 * FFN_HIDDEN),
        "l0_ffn_w_down": dense(FFN_HIDDEN, D_MODEL),
        "l1_ada_w": dense(D_MODEL, 6 * D_MODEL, 0.5),
        "l1_ada_b": nrm((6 * D_MODEL,), 0.02),
        "l1_norm1": gain(D_MODEL),
        "l1_mlstm_w_in": dense(D_MODEL, MLSTM_IN_WIDTH),
        "l1_mlstm_b_gates": b_gates,
        "l1_mlstm_head_norm": gain(MLSTM_HEADS * MLSTM_V_DIM),
        "l1_mlstm_w_out": dense(MLSTM_HEADS * MLSTM_V_DIM, D_MODEL),
        "l1_norm2": gain(D_MODEL),
        "l1_ffn_w_gu": dense(D_MODEL, 2 * FFN_HIDDEN),
        "l1_ffn_w_down": dense(FFN_HIDDEN, D_MODEL),
        "final_norm": gain(D_MODEL),
    }


def reference(x_prompt, x_sample, c_prompt, c_sample,
              l0_ada_w, l0_ada_b, l0_norm1, l0_attn_w_qkv, l0_attn_w_o, l0_norm2, l0_ffn_w_gu, l0_ffn_w_down,
              l1_ada_w, l1_ada_b, l1_norm1, l1_mlstm_w_in, l1_mlstm_b_gates, l1_mlstm_head_norm, l1_mlstm_w_out,
              l1_norm2, l1_ffn_w_gu, l1_ffn_w_down, final_norm):
    mixers = [
        lambda h: dilated_attention_mixer(h, l0_attn_w_qkv, l0_attn_w_o),
        lambda h: mlstm_mixer(h, l1_mlstm_w_in, l1_mlstm_b_gates, l1_mlstm_head_norm, l1_mlstm_w_out),
    ]
    layers = [
        (l0_ada_w, l0_ada_b, l0_norm1, l0_norm2, l0_ffn_w_gu, l0_ffn_w_down),
        (l1_ada_w, l1_ada_b, l1_norm1, l1_norm2, l1_ffn_w_gu, l1_ffn_w_down),
    ]
    y_prompt = run_trunk(x_prompt, c_prompt, layers, mixers, final_norm)
    y_sample = run_trunk(x_sample, c_sample, layers, mixers, final_norm)
    return (y_prompt, y_sample)
```

```python
import functools
import math

import jax
import jax.numpy as jnp
from jax import lax
from jax.experimental import pallas as pl
from jax.experimental.pallas import tpu as pltpu

F32 = jnp.float32
BF16 = jnp.bfloat16

D_MODEL = 1024
ATT_GROUPS = ((128, 1), (512, 4), (2048, 16))
ATT_HEADS = 8
ATT_HEAD_DIM = 64
ATT_GROUP_WIDTH = ATT_HEADS * ATT_HEAD_DIM
ATT_QKV_WIDTH = len(ATT_GROUPS) * 3 * ATT_GROUP_WIDTH
ATT_RADIUS = ATT_GROUPS[0][0] // (2 * ATT_GROUPS[0][1])
assert all(w // (2 * d) == ATT_RADIUS for w, d in ATT_GROUPS)
ROPE_DIM = ATT_HEAD_DIM // 4
ROPE_HALF = ROPE_DIM // 2
ROPE_THETA = 500000.0
MLSTM_HEADS = 4
MLSTM_QK_DIM = D_MODEL // (2 * MLSTM_HEADS)
MLSTM_V_DIM = D_MODEL // MLSTM_HEADS
MLSTM_QK_WIDTH = MLSTM_HEADS * MLSTM_QK_DIM
MLSTM_V_WIDTH = MLSTM_HEADS * MLSTM_V_DIM
MLSTM_MAIN_WIDTH = 2 * MLSTM_QK_WIDTH + 2 * MLSTM_V_WIDTH
MLSTM_GATES = 4 * MLSTM_HEADS
FFN_HIDDEN = int(math.ceil(8 * D_MODEL / 3 / 256)) * 256
RMS_EPS = 1e-6

LANES = 128
TOKEN_TILE = 512
ATT_Q_BLOCK = 128
ATT_ROWS = 256
SCAN_CHUNK = 128
SCAN_ROWS = 512
FFN_CHUNK = 512
NEG_BIG = -1e30
VMEM_LIMIT = 56 * 1024 * 1024


def _params(semantics):
    return pltpu.CompilerParams(dimension_semantics=semantics, vmem_limit_bytes=VMEM_LIMIT)


def _const_spec(shape):
    zeros = (0,) * len(shape)
    return pl.BlockSpec(shape, lambda *_: zeros, pipeline_mode=pl.Buffered(1))


def _split3(x):
    a = x.astype(BF16)
    r = x - a.astype(F32)
    b = r.astype(BF16)
    c = (r - b.astype(F32)).astype(BF16)
    return a, b, c


def _dot(a, b):
    return jnp.dot(a, b, preferred_element_type=F32)


def _norm_mod(x, gain, scale, shift):
    y = x * lax.rsqrt(jnp.mean(x * x, axis=-1, keepdims=True) + RMS_EPS)
    return (y * gain) * (1.0 + scale) + shift


def _ada_kernel(c_ref, w_ref, b_ref, o_ref):
    c = c_ref[...]
    a = c * jax.nn.sigmoid(c)
    a1, a2, _ = _split3(a)
    w1, w2, _ = _split3(w_ref[...])
    o_ref[...] = _dot(a1, w1) + _dot(a2, w1) + _dot(a1, w2) + b_ref[...]


def _ada_mod(c, ada_w, ada_b):
    nb = c.shape[0]
    width = ada_w.shape[1]
    tile = width // 4
    out = pl.pallas_call(
        _ada_kernel,
        grid=(4,),
        in_specs=[
            pl.BlockSpec((nb, D_MODEL), lambda j: (0, 0)),
            pl.BlockSpec((D_MODEL, tile), lambda j: (0, j)),
            pl.BlockSpec((1, tile), lambda j: (0, j)),
        ],
        out_specs=pl.BlockSpec((nb, tile), lambda j: (0, j)),
        out_shape=jax.ShapeDtypeStruct((nb, width), F32),
        compiler_params=_params(("arbitrary",)),
        name="ada_mod",
    )(c, ada_w, ada_b.reshape(1, width))
    return out.reshape(nb, 6, D_MODEL)


def _rope_tables(seq):
    inv = jnp.float32(ROPE_THETA) ** (-jnp.arange(ROPE_HALF, dtype=F32) / ROPE_HALF)
    ang = jnp.arange(seq, dtype=F32)[:, None] * inv[None, :]
    cos, sin = jnp.cos(ang), jnp.sin(ang)
    rest = ATT_HEAD_DIM - ROPE_DIM
    zeros = jnp.zeros((seq, ROPE_HALF), F32)
    c = jnp.concatenate([cos, cos, jnp.ones((seq, rest), F32)], axis=1)
    s_up = jnp.concatenate([-sin, zeros, jnp.zeros((seq, rest), F32)], axis=1)
    s_dn = jnp.concatenate([zeros, sin, jnp.zeros((seq, rest), F32)], axis=1)
    reps = LANES // ATT_HEAD_DIM
    return tuple(jnp.tile(t, (1, reps)) for t in (c, s_up, s_dn))


def _qkv_kernel(x_ref, mod_ref, g_ref, w_ref, c_ref, su_ref, sd_ref, o_ref):
    mod = mod_ref[0]
    h = _norm_mod(x_ref[...], g_ref[...], mod[1:2], mod[0:1]).astype(BF16)
    cos, s_up, s_dn = c_ref[...], su_ref[...], sd_ref[...]
    gw = ATT_GROUP_WIDTH
    for blk in range(ATT_QKV_WIDTH // gw):
        p = _dot(h, w_ref[:, blk * gw:(blk + 1) * gw])
        kind = blk % 3
        if kind == 2:
            o_ref[:, blk * gw:(blk + 1) * gw] = p.astype(BF16)
            continue
        for s in range(gw // LANES):
            ps = p[:, s * LANES:(s + 1) * LANES]
            rot = ps * cos + pltpu.roll(ps, LANES - ROPE_HALF, 1) * s_up + pltpu.roll(ps, ROPE_HALF, 1) * s_dn
            if kind == 0:
                rot = rot * (ATT_HEAD_DIM ** -0.5)
            lo = blk * gw + s * LANES
            o_ref[:, lo:lo + LANES] = rot.astype(BF16)


def _qkv_proj(x2, mod, gain, w_qkv, tables, seq):
    tokens = x2.shape[0]
    tm = TOKEN_TILE
    per_row = seq // tm
    tab_spec = pl.BlockSpec((tm, LANES), lambda i: (i % per_row, 0))
    return pl.pallas_call(
        _qkv_kernel,
        grid=(tokens // tm,),
        in_specs=[
            pl.BlockSpec((tm, D_MODEL), lambda i: (i, 0)),
            pl.BlockSpec((1, 6, D_MODEL), lambda i: (i // per_row, 0, 0)),
            _const_spec((1, D_MODEL)),
            _const_spec((D_MODEL, ATT_QKV_WIDTH)),
            tab_spec, tab_spec, tab_spec,
        ],
        out_specs=pl.BlockSpec((tm, ATT_QKV_WIDTH), lambda i: (i, 0)),
        out_shape=jax.ShapeDtypeStruct((tokens, ATT_QKV_WIDTH), BF16),
        compiler_params=_params(("parallel",)),
        name="qkv_proj",
    )(x2, mod, gain, w_qkv, *tables)


def _attn_kernel(q_ref, k_ref, kp_ref, kn_ref, v_ref, vp_ref, vn_ref, o_ref, lse_ref, kbuf, vbuf, *, length, rows):
    rad, bq = ATT_RADIUS, ATT_Q_BLOCK
    nk = bq + 2 * rad
    i = pl.program_id(2)
    kbuf[0:rad, :] = kp_ref[0]
    kbuf[rad:rad + rows, :] = k_ref[0]
    kbuf[rad + rows:, :] = kn_ref[0]
    vbuf[0:rad, :] = vp_ref[0]
    vbuf[rad:rad + rows, :] = v_ref[0]
    vbuf[rad + rows:, :] = vn_ref[0]

    lane = lax.broadcasted_iota(jnp.int32, (bq, LANES), 1)
    first = lane < ATT_HEAD_DIM
    head_mask = (jnp.where(first, 1.0, 0.0).astype(BF16), jnp.where(first, 0.0, 1.0).astype(BF16))
    t_idx = lax.broadcasted_iota(jnp.int32, (bq, nk), 0)
    c_idx = lax.broadcasted_iota(jnp.int32, (bq, nk), 1)
    rel = c_idx - rad - t_idx
    for jb in range(rows // bq):
        kpos = i * rows + jb * bq - rad + c_idx
        ok = (jnp.abs(rel) <= rad) & (kpos >= 0) & (kpos < length)
        bias = jnp.where(ok, 0.0, NEG_BIG)
        for hp in range(ATT_GROUP_WIDTH // LANES):
            cols = slice(hp * LANES, (hp + 1) * LANES)
            q2 = q_ref[0, jb * bq:(jb + 1) * bq, cols]
            k2 = kbuf[jb * bq:jb * bq + nk, cols]
            v2 = vbuf[jb * bq:jb * bq + nk, cols]
            outs, lses = [], []
            for half in range(2):
                s = lax.dot_general(q2 * head_mask[half], k2, (((1,), (1,)), ((), ())),
                                    preferred_element_type=F32) + bias
                m = jnp.max(s, axis=1, keepdims=True)
                p = jnp.exp(s - m)
                den = jnp.sum(p, axis=1, keepdims=True)
                outs.append(_dot(p.astype(BF16), v2) / den)
                lses.append(m + jnp.log(den))
            o_ref[0, jb * bq:(jb + 1) * bq, cols] = jnp.where(first, outs[0], outs[1]).astype(BF16)
            lse_ref[0, jb * bq:(jb + 1) * bq, cols] = jnp.where(first, lses[0], lses[1])


def _attention_group(qkv, batch, seq, group):
    _, dil = ATT_GROUPS[group]
    length = seq // dil
    rows, rad, gw = min(ATT_ROWS, length), ATT_RADIUS, ATT_GROUP_WIDTH
    assert length % rows == 0 and rows % ATT_Q_BLOCK == 0
    nblk = length // rows
    halo_per_blk = rows // rad
    last_halo = length // rad - 1
    ncol = ATT_QKV_WIDTH // gw
    view = qkv.reshape(batch, length, dil * ATT_QKV_WIDTH)
    base = 3 * group

    def main(kind):
        return pl.BlockSpec((1, rows, gw), lambda b, p, i: (b, i, p * ncol + base + kind))

    def prev(kind):
        return pl.BlockSpec((1, rad, gw), lambda b, p, i: (b, jnp.maximum(i * halo_per_blk - 1, 0), p * ncol + base + kind))

    def nxt(kind):
        return pl.BlockSpec((1, rad, gw),
                            lambda b, p, i: (b, jnp.minimum((i + 1) * halo_per_blk, last_halo), p * ncol + base + kind))

    out_spec = pl.BlockSpec((1, rows, gw), lambda b, p, i: (b, i, p))
    o, lse = pl.pallas_call(
        functools.partial(_attn_kernel, length=length, rows=rows),
        grid=(batch, dil, nblk),
        in_specs=[main(0), main(1), prev(1), nxt(1), main(2), prev(2), nxt(2)],
        out_specs=[out_spec, out_spec],
        out_shape=[jax.ShapeDtypeStruct((batch, length, dil * gw), BF16),
                   jax.ShapeDtypeStruct((batch, length, dil * gw), F32)],
        scratch_shapes=[pltpu.VMEM((rows + 2 * rad, gw), BF16), pltpu.VMEM((rows + 2 * rad, gw), BF16)],
        compiler_params=_params(("parallel", "parallel", "parallel")),
        name=f"attn_g{group}",
    )(view, view, view, view, view, view, view)
    return o.reshape(batch * seq, gw), lse.reshape(batch * seq, gw)


def _ffn(x1, mod, g2, wgu_ref, wd_ref, act_ref):
    h = _norm_mod(x1, g2, mod[4:5], mod[3:4]).astype(BF16)
    lo = 0
    while lo < FFN_HIDDEN:
        ck = min(FFN_CHUNK, FFN_HIDDEN - lo)
        a = _dot(h, wgu_ref[:, lo:lo + ck])
        b = _dot(h, wgu_ref[:, FFN_HIDDEN + lo:FFN_HIDDEN + lo + ck])
        act_ref[:, lo:lo + ck] = (a * jax.nn.sigmoid(a) * b).astype(BF16)
        lo += ck
    return x1 + mod[5:6] * _dot(act_ref[...], wd_ref[...])


def _l0_post_kernel(x_ref, o0_ref, o1_ref, o2_ref, l0_ref, l1_ref, l2_ref, mod_ref, g2_ref,
                    wo_ref, wgu_ref, wd_ref, out_ref, act_ref):
    mod = mod_ref[0]
    l0, l1, l2 = l0_ref[...], l1_ref[...], l2_ref[...]
    top = jnp.maximum(jnp.maximum(l0, l1), l2)
    e0, e1, e2 = jnp.exp(l0 - top), jnp.exp(l1 - top), jnp.exp(l2 - top)
    y = (e0 * o0_ref[...].astype(F32) + e1 * o1_ref[...].astype(F32) + e2 * o2_ref[...].astype(F32)) / (e0 + e1 + e2)
    x1 = x_ref[...] + mod[2:3] * _dot(y.astype(BF16), wo_ref[...])
    out_ref[...] = _ffn(x1, mod, g2_ref[...], wgu_ref, wd_ref, act_ref)


def _l0_post(x2, outs, lses, mod, g2, w_o, w_gu, w_down, seq):
    tokens = x2.shape[0]
    tm = TOKEN_TILE
    per_row = seq // tm
    gw = ATT_GROUP_WIDTH
    tile = lambda w: pl.BlockSpec((tm, w), lambda i: (i, 0))
    return pl.pallas_call(
        _l0_post_kernel,
        grid=(tokens // tm,),
        in_specs=[tile(D_MODEL)] + [tile(gw)] * 6 + [
            pl.BlockSpec((1, 6, D_MODEL), lambda i: (i // per_row, 0, 0)),
            _const_spec((1, D_MODEL)),
            _const_spec((gw, D_MODEL)),
            _const_spec((D_MODEL, 2 * FFN_HIDDEN)),
            _const_spec((FFN_HIDDEN, D_MODEL)),
        ],
        out_specs=tile(D_MODEL),
        out_shape=jax.ShapeDtypeStruct((tokens, D_MODEL), F32),
        scratch_shapes=[pltpu.VMEM((tm, FFN_HIDDEN), BF16)],
        compiler_params=_params(("parallel",)),
        name="l0_post",
    )(x2, *outs, *lses, mod, g2, w_o, w_gu, w_down)


def _mlstm_in_kernel(x_ref, mod_ref, g_ref, w_ref, wg1_ref, wg2_ref, bg_ref,
                     q_ref, k_ref, v_ref, og_ref, gates_ref):
    mod = mod_ref[0]
    h32 = _norm_mod(x_ref[...], g_ref[...], mod[1:2], mod[0:1])
    h = h32.astype(BF16)
    h_lo = (h32 - h.astype(F32)).astype(BF16)
    qw, vw = MLSTM_QK_WIDTH, MLSTM_V_WIDTH
    q_ref[...] = _dot(h, w_ref[:, 0:qw]).astype(BF16)
    k_ref[...] = (_dot(h, w_ref[:, qw:2 * qw]) * (MLSTM_QK_DIM ** -0.5)).astype(BF16)
    v_ref[...] = _dot(h, w_ref[:, 2 * qw:2 * qw + vw]).astype(BF16)
    og_ref[...] = jax.nn.sigmoid(_dot(h, w_ref[:, 2 * qw + vw:2 * qw + 2 * vw])).astype(BF16)
    g = _dot(h, wg1_ref[...]) + _dot(h_lo, wg1_ref[...]) + _dot(h, wg2_ref[...]) + bg_ref[...]
    lane = lax.broadcasted_iota(jnp.int32, g.shape, 1)
    is_forget = (((lane >> 2) & 1) == 1) & (lane < MLSTM_GATES)
    log_sig = jnp.minimum(g, 0.0) - jnp.log(1.0 + jnp.exp(-jnp.abs(g)))
    gates_ref[...] = jnp.where(is_forget, log_sig, g)


def _mlstm_in(x2, mod, gain, w_main, wg1, wg2, bg, seq):
    tokens = x2.shape[0]
    tm = TOKEN_TILE
    per_row = seq // tm
    tile = lambda w: pl.BlockSpec((tm, w), lambda i: (i, 0))
    qw, vw = MLSTM_QK_WIDTH, MLSTM_V_WIDTH
    return pl.pallas_call(
        _mlstm_in_kernel,
        grid=(tokens // tm,),
        in_specs=[
            tile(D_MODEL),
            pl.BlockSpec((1, 6, D_MODEL), lambda i: (i // per_row, 0, 0)),
            _const_spec((1, D_MODEL)),
            _const_spec((D_MODEL, MLSTM_MAIN_WIDTH)),
            _const_spec((D_MODEL, LANES)),
            _const_spec((D_MODEL, LANES)),
            _const_spec((1, LANES)),
        ],
        out_specs=[tile(qw), tile(qw), tile(vw), tile(vw), tile(LANES)],
        out_shape=[
            jax.ShapeDtypeStruct((tokens, qw), BF16),
            jax.ShapeDtypeStruct((tokens, qw), BF16),
            jax.ShapeDtypeStruct((tokens, vw), BF16),
            jax.ShapeDtypeStruct((tokens, vw), BF16),
            jax.ShapeDtypeStruct((tokens, LANES), F32),
        ],
        compiler_params=_params(("parallel",)),
        name="mlstm_in",
    )(x2, mod, gain, w_main, wg1, wg2, bg)


def _scan_chunk(q_ref, k_ref, v_ref, gates_ref, c_ref, n_ref, m_ref, r0, reverse, emit):
    cl, dk, dv = SCAN_CHUNK, MLSTM_QK_DIM, MLSTM_V_DIM
    direction = 1 if reverse else 0
    g = gates_ref[r0:r0 + cl, :]
    row = lax.broadcasted_iota(jnp.int32, (cl, cl), 0)
    col = lax.broadcasted_iota(jnp.int32, (cl, cl), 1)
    keep = (col >= row) if reverse else (col <= row)
    tri = jnp.where(keep, 1.0, 0.0).astype(BF16)
    g1, g2, g3 = _split3(g)
    csum = _dot(tri, g1) + _dot(tri, g2) + _dot(tri, g3)
    g_t = g.T
    csum_t = csum.T
    last = 0 if reverse else cl - 1
    for h in range(MLSTM_HEADS):
        ci = 8 * direction + h
        cf = ci + MLSTM_HEADS
        i_col, i_row = g[:, ci:ci + 1], g_t[ci:ci + 1, :]
        b_col, b_row = csum[:, cf:cf + 1], csum_t[cf:cf + 1, :]
        m_prev = m_ref[h][0:1, 0:1]
        q = q_ref[r0:r0 + cl, h * dk:(h + 1) * dk]
        k = k_ref[r0:r0 + cl, h * dk:(h + 1) * dk]
        v = v_ref[r0:r0 + cl, h * dv:(h + 1) * dv]
        c_prev = c_ref[h]
        n_prev = n_ref[h][0:1, :]

        dmat = jnp.where(keep, b_col - b_row + i_row, NEG_BIG)
        m_inter = b_col + m_prev
        m_t = jnp.maximum(m_inter, jnp.max(dmat, axis=1, keepdims=True))
        wts = jnp.exp(dmat - m_t)
        inter = jnp.exp(m_inter - m_t)
        sqk = lax.dot_general(q, k, (((1,), (1,)), ((), ())), preferred_element_type=F32) * wts
        num = _dot(sqk.astype(BF16), v) + inter * _dot(q, c_prev.astype(BF16))
        den = jnp.sum(sqk, axis=1, keepdims=True) + inter * jnp.sum(q.astype(F32) * n_prev, axis=1, keepdims=True)
        emit(h, num / jnp.maximum(jnp.abs(den), jnp.exp(-m_t)))

        b_last = b_col[last:last + 1, :]
        g_col = b_last - b_col + i_col
        m_new = jnp.maximum(b_last + m_prev, jnp.max(g_col, axis=0, keepdims=True))
        decay = jnp.exp(b_last + m_prev - m_new)
        kw = k.astype(F32) * jnp.exp(g_col - m_new)
        c_ref[h] = decay * c_prev + _dot(kw.T.astype(BF16), v)
        n_ref[h] = jnp.broadcast_to(decay * n_prev + jnp.sum(kw, axis=0, keepdims=True), n_ref.shape[1:])
        m_ref[h] = jnp.broadcast_to(m_new, m_ref.shape[1:])


def _scan_init(c_ref, n_ref, m_ref):
    @pl.when(pl.program_id(1) == 0)
    def _():
        c_ref[...] = jnp.zeros_like(c_ref)
        n_ref[...] = jnp.zeros_like(n_ref)
        m_ref[...] = jnp.zeros_like(m_ref)


def _scan_bwd_kernel(q_ref, k_ref, v_ref, gates_ref, h_ref, c_ref, n_ref, m_ref):
    _scan_init(c_ref, n_ref, m_ref)
    dv = MLSTM_V_DIM
    for cidx in reversed(range(SCAN_ROWS // SCAN_CHUNK)):
        r0 = cidx * SCAN_CHUNK

        def emit(h, val, r0=r0):
            h_ref[r0:r0 + SCAN_CHUNK, h * dv:(h + 1) * dv] = val

        _scan_chunk(q_ref, k_ref, v_ref, gates_ref, c_ref, n_ref, m_ref, r0, True, emit)


def _scan_fwd_kernel(q_ref, k_ref, v_ref, gates_ref, hb_ref, og_ref, hn_ref, y_ref, c_ref, n_ref, m_ref):
    _scan_init(c_ref, n_ref, m_ref)
    dv = MLSTM_V_DIM
    for cidx in range(SCAN_ROWS // SCAN_CHUNK):
        r0 = cidx * SCAN_CHUNK

        def emit(h, val, r0=r0):
            rows, cols = slice(r0, r0 + SCAN_CHUNK), slice(h * dv, (h + 1) * dv)
            hs = val + hb_ref[rows, cols]
            hs = hs * lax.rsqrt(jnp.mean(hs * hs, axis=-1, keepdims=True) + RMS_EPS) * hn_ref[:, cols]
            y_ref[rows, cols] = (og_ref[rows, cols].astype(F32) * hs).astype(BF16)

        _scan_chunk(q_ref, k_ref, v_ref, gates_ref, c_ref, n_ref, m_ref, r0, False, emit)


def _scan_scratch():
    return [
        pltpu.VMEM((MLSTM_HEADS, MLSTM_QK_DIM, MLSTM_V_DIM), F32),
        pltpu.VMEM((MLSTM_HEADS, 8, MLSTM_QK_DIM), F32),
        pltpu.VMEM((MLSTM_HEADS, 8, LANES), F32),
    ]


def _mlstm_scans(q, k, v, gates, og, head_norm, batch, seq):
    rows = SCAN_ROWS
    per_row = seq // rows
    qw, vw = MLSTM_QK_WIDTH, MLSTM_V_WIDTH
    tokens = batch * seq

    def spec(width, reverse):
        if reverse:
            return pl.BlockSpec((rows, width), lambda b, i: (b * per_row + per_row - 1 - i, 0))
        return pl.BlockSpec((rows, width), lambda b, i: (b * per_row + i, 0))

    params = _params(("parallel", "arbitrary"))
    h_bw = pl.pallas_call(
        _scan_bwd_kernel,
        grid=(batch, per_row),
        in_specs=[spec(qw, True), spec(qw, True), spec(vw, True), spec(LANES, True)],
        out_specs=spec(vw, True),
        out_shape=jax.ShapeDtypeStruct((tokens, vw), F32),
        scratch_shapes=_scan_scratch(),
        compiler_params=params,
        name="mlstm_bwd",
    )(q, k, v, gates)
    return pl.pallas_call(
        _scan_fwd_kernel,
        grid=(batch, per_row),
        in_specs=[spec(qw, False), spec(qw, False), spec(vw, False), spec(LANES, False),
                  spec(vw, False), spec(vw, False), _const_spec((1, vw))],
        out_specs=spec(vw, False),
        out_shape=jax.ShapeDtypeStruct((tokens, vw), BF16),
        scratch_shapes=_scan_scratch(),
        compiler_params=params,
        name="mlstm_fwd",
    )(q, k, v, gates, h_bw, og, head_norm)


def _l1_post_kernel(x_ref, y_ref, mod_ref, g2_ref, gf_ref, wout_ref, wgu_ref, wd_ref, out_ref, act_ref):
    mod = mod_ref[0]
    x1 = x_ref[...] + mod[2:3] * _dot(y_ref[...], wout_ref[...])
    x2 = _ffn(x1, mod, g2_ref[...], wgu_ref, wd_ref, act_ref)
    out_ref[...] = x2 * lax.rsqrt(jnp.mean(x2 * x2, axis=-1, keepdims=True) + RMS_EPS) * gf_ref[...]


def _l1_post(x2, y, mod, g2, g_final, w_out, w_gu, w_down, seq):
    tokens = x2.shape[0]
    tm = TOKEN_TILE
    per_row = seq // tm
    tile = lambda w: pl.BlockSpec((tm, w), lambda i: (i, 0))
    return pl.pallas_call(
        _l1_post_kernel,
        grid=(tokens // tm,),
        in_specs=[
            tile(D_MODEL), tile(MLSTM_V_WIDTH),
            pl.BlockSpec((1, 6, D_MODEL), lambda i: (i // per_row, 0, 0)),
            _const_spec((1, D_MODEL)),
            _const_spec((1, D_MODEL)),
            _const_spec((MLSTM_V_WIDTH, D_MODEL)),
            _const_spec((D_MODEL, 2 * FFN_HIDDEN)),
            _const_spec((FFN_HIDDEN, D_MODEL)),
        ],
        out_specs=tile(D_MODEL),
        out_shape=jax.ShapeDtypeStruct((tokens, D_MODEL), F32),
        scratch_shapes=[pltpu.VMEM((tm, FFN_HIDDEN), BF16)],
        compiler_params=_params(("parallel",)),
        name="l1_post",
    )(x2, y, mod, g2, g_final, w_out, w_gu, w_down)


def _trunk(x, mod0, mod1, weights, tables):
    batch, seq, _ = x.shape
    assert seq % TOKEN_TILE == 0 and seq % SCAN_ROWS == 0
    x2 = x.reshape(batch * seq, D_MODEL)
    qkv = _qkv_proj(x2, mod0, weights["l0_norm1"], weights["w_qkv"], tables, seq)
    outs, lses = zip(*[_attention_group(qkv, batch, seq, g) for g in range(len(ATT_GROUPS))])
    x2 = _l0_post(x2, outs, lses, mod0, weights["l0_norm2"], weights["w_o"], weights["l0_w_gu"],
                  weights["l0_w_down"], seq)
    q, k, v, og, gates = _mlstm_in(x2, mod1, weights["l1_norm1"], weights["w_in"], weights["wg1"],
                                   weights["wg2"], weights["b_gates"], seq)
    y = _mlstm_scans(q, k, v, gates, og, weights["head_norm"], batch, seq)
    out = _l1_post(x2, y, mod1, weights["l1_norm2"], weights["final_norm"], weights["w_out"],
                   weights["l1_w_gu"], weights["l1_w_down"], seq)
    return out.reshape(batch, seq, D_MODEL)


def kernel(x_prompt, x_sample, c_prompt, c_sample, l0_ada_w, l0_ada_b, l0_norm1, l0_attn_w_qkv, l0_attn_w_o, l0_norm2, l0_ffn_w_gu, l0_ffn_w_down, l1_ada_w, l1_ada_b, l1_norm1, l1_mlstm_w_in, l1_mlstm_b_gates, l1_mlstm_head_norm, l1_mlstm_w_out, l1_norm2, l1_ffn_w_gu, l1_ffn_w_down, final_norm):
    row = lambda g: g.reshape(1, -1).astype(F32)
    w_gate = jnp.pad(l1_mlstm_w_in[:, MLSTM_MAIN_WIDTH:], ((0, 0), (0, LANES - MLSTM_GATES)))
    wg1 = w_gate.astype(BF16)
    weights = {
        "l0_norm1": row(l0_norm1), "l0_norm2": row(l0_norm2), "l1_norm1": row(l1_norm1), "l1_norm2": row(l1_norm2),
        "final_norm": row(final_norm), "head_norm": row(l1_mlstm_head_norm),
        "w_qkv": l0_attn_w_qkv.astype(BF16), "w_o": l0_attn_w_o.astype(BF16),
        "l0_w_gu": l0_ffn_w_gu.astype(BF16), "l0_w_down": l0_ffn_w_down.astype(BF16),
        "w_in": l1_mlstm_w_in[:, :MLSTM_MAIN_WIDTH].astype(BF16),
        "wg1": wg1, "wg2": (w_gate - wg1.astype(F32)).astype(BF16),
        "b_gates": jnp.pad(l1_mlstm_b_gates, (0, LANES - MLSTM_GATES)).reshape(1, LANES).astype(F32),
        "w_out": l1_mlstm_w_out.astype(BF16),
        "l1_w_gu": l1_ffn_w_gu.astype(BF16), "l1_w_down": l1_ffn_w_down.astype(BF16),
    }
    nb = x_prompt.shape[0]
    c_all = jnp.concatenate([c_prompt, c_sample], axis=0)
    mod0 = _ada_mod(c_all, l0_ada_w, l0_ada_b)
    mod1 = _ada_mod(c_all, l1_ada_w, l1_ada_b)
    tables = _rope_tables(max(x_prompt.shape[1], x_sample.shape[1]))
    y_prompt = _trunk(x_prompt, mod0[:nb], mod1[:nb], weights, tables)
    y_sample = _trunk(x_sample, mod0[nb:], mod1[nb:], weights, tables)
    return (y_prompt, y_sample)
```

```python
import functools
import math

import jax
import jax.numpy as jnp
from jax import lax
from jax.experimental import pallas as pl
from jax.experimental.pallas import tpu as pltpu

F32 = jnp.float32
BF16 = jnp.bfloat16

D_MODEL = 1024
ATT_GROUPS = ((128, 1), (512, 4), (2048, 16))
ATT_HEADS = 8
ATT_HEAD_DIM = 64
ATT_GROUP_WIDTH = ATT_HEADS * ATT_HEAD_DIM
ATT_QKV_WIDTH = len(ATT_GROUPS) * 3 * ATT_GROUP_WIDTH
ATT_RADIUS = ATT_GROUPS[0][0] // (2 * ATT_GROUPS[0][1])
assert all(w // (2 * d) == ATT_RADIUS for w, d in ATT_GROUPS)
ROPE_DIM = ATT_HEAD_DIM // 4
ROPE_HALF = ROPE_DIM // 2
ROPE_THETA = 500000.0
MLSTM_HEADS = 4
MLSTM_QK_DIM = D_MODEL // (2 * MLSTM_HEADS)
MLSTM_V_DIM = D_MODEL // MLSTM_HEADS
MLSTM_QK_WIDTH = MLSTM_HEADS * MLSTM_QK_DIM
MLSTM_V_WIDTH = MLSTM_HEADS * MLSTM_V_DIM
MLSTM_MAIN_WIDTH = 2 * MLSTM_QK_WIDTH + 2 * MLSTM_V_WIDTH
MLSTM_GATES = 4 * MLSTM_HEADS
FFN_HIDDEN = int(math.ceil(8 * D_MODEL / 3 / 256)) * 256
RMS_EPS = 1e-6

LANES = 128
TOKEN_TILE = 512
ATT_Q_BLOCK = 128
ATT_ROWS = 256
SCAN_CHUNK = 128
SCAN_ROWS = 512
FFN_CHUNK = 512
NEG_BIG = -1e30
VMEM_LIMIT = 56 * 1024 * 1024


def _params(semantics):
    return pltpu.CompilerParams(dimension_semantics=semantics, vmem_limit_bytes=VMEM_LIMIT)


def _const_spec(shape):
    zeros = (0,) * len(shape)
    return pl.BlockSpec(shape, lambda *_: zeros, pipeline_mode=pl.Buffered(1))


def _split3(x):
    a = x.astype(BF16)
    r = x - a.astype(F32)
    b = r.astype(BF16)
    c = (r - b.astype(F32)).astype(BF16)
    return a, b, c


def _dot(a, b):
    return jnp.dot(a, b, preferred_element_type=F32)


def _norm_mod(x, gain, scale, shift):
    y = x * lax.rsqrt(jnp.mean(x * x, axis=-1, keepdims=True) + RMS_EPS)
    return (y * gain) * (1.0 + scale) + shift


def _ada_kernel(c_ref, w_ref, b_ref, o_ref):
    c = c_ref[...]
    a = c * jax.nn.sigmoid(c)
    a1, a2, _ = _split3(a)
    w1, w2, _ = _split3(w_ref[...])
    o_ref[...] = _dot(a1, w1) + _dot(a2, w1) + _dot(a1, w2) + b_ref[...]


def _ada_mod(c, ada_w, ada_b):
    nb = c.shape[0]
    width = ada_w.shape[1]
    tile = width // 4
    out = pl.pallas_call(
        _ada_kernel,
        grid=(4,),
        in_specs=[
            pl.BlockSpec((nb, D_MODEL), lambda j: (0, 0)),
            pl.BlockSpec((D_MODEL, tile), lambda j: (0, j)),
            pl.BlockSpec((1, tile), lambda j: (0, j)),
        ],
        out_specs=pl.BlockSpec((nb, tile), lambda j: (0, j)),
        out_shape=jax.ShapeDtypeStruct((nb, width), F32),
        compiler_params=_params(("arbitrary",)),
        name="ada_mod",
    )(c, ada_w, ada_b.reshape(1, width))
    return out.reshape(nb, 6, D_MODEL)


def _rope_tables(seq):
    inv = jnp.float32(ROPE_THETA) ** (-jnp.arange(ROPE_HALF, dtype=F32) / ROPE_HALF)
    ang = jnp.arange(seq, dtype=F32)[:, None] * inv[None, :]
    cos, sin = jnp.cos(ang), jnp.sin(ang)
    rest = ATT_HEAD_DIM - ROPE_DIM
    zeros = jnp.zeros((seq, ROPE_HALF), F32)
    c = jnp.concatenate([cos, cos, jnp.ones((seq, rest), F32)], axis=1)
    s_up = jnp.concatenate([-sin, zeros, jnp.zeros((seq, rest), F32)], axis=1)
    s_dn = jnp.concatenate([zeros, sin, jnp.zeros((seq, rest), F32)], axis=1)
    reps = LANES // ATT_HEAD_DIM
    return tuple(jnp.tile(t, (1, reps)) for t in (c, s_up, s_dn))


def _qkv_kernel(x_ref, mod_ref, g_ref, w_ref, c_ref, su_ref, sd_ref, o0_ref, o1_ref, o2_ref, stage_ref):
    mod = mod_ref[0]
    h = _norm_mod(x_ref[...], g_ref[...], mod[1:2], mod[0:1]).astype(BF16)
    cos, s_up, s_dn = c_ref[...], su_ref[...], sd_ref[...]
    gw = ATT_GROUP_WIDTH
    tm = x_ref.shape[0]
    slab = 0
    for blk in range(ATT_QKV_WIDTH // gw):
        grp, kind = divmod(blk, 3)
        dil = ATT_GROUPS[grp][1]
        o_ref = (o0_ref, o1_ref, o2_ref)[grp]
        p = _dot(h, w_ref[:, blk * gw:(blk + 1) * gw])
        for s in range(gw // LANES):
            val = p[:, s * LANES:(s + 1) * LANES]
            if kind < 2:
                val = val * cos + pltpu.roll(val, LANES - ROPE_HALF, 1) * s_up + pltpu.roll(val, ROPE_HALF, 1) * s_dn
            if kind == 0:
                val = val * (ATT_HEAD_DIM ** -0.5)
            lo = kind * gw + s * LANES
            if dil == 1:
                o_ref[0, :, lo:lo + LANES] = val.astype(BF16)
                continue
            stage_ref[slab] = val
            for ph in range(dil):
                rows = stage_ref[slab, pl.ds(ph, tm // dil, stride=dil), :]
                o_ref[0, :, ph * 3 * gw + lo:ph * 3 * gw + lo + LANES] = rows.astype(BF16)
            slab += 1


def _qkv_proj(x2, mod, gain, w_qkv, tables, batch, seq):
    tokens = x2.shape[0]
    tm = TOKEN_TILE
    per_row = seq // tm
    tab_spec = pl.BlockSpec((tm, LANES), lambda i: (i % per_row, 0))
    gw3 = 3 * ATT_GROUP_WIDTH
    dils = [d for _, d in ATT_GROUPS]
    n_stage = sum(gw3 // LANES for d in dils if d > 1)
    return pl.pallas_call(
        _qkv_kernel,
        grid=(tokens // tm,),
        in_specs=[
            pl.BlockSpec((tm, D_MODEL), lambda i: (i, 0)),
            pl.BlockSpec((1, 6, D_MODEL), lambda i: (i // per_row, 0, 0)),
            _const_spec((1, D_MODEL)),
            _const_spec((D_MODEL, ATT_QKV_WIDTH)),
            tab_spec, tab_spec, tab_spec,
        ],
        out_specs=[pl.BlockSpec((1, tm // d, d * gw3), lambda i: (i // per_row, i % per_row, 0)) for d in dils],
        out_shape=[jax.ShapeDtypeStruct((batch, seq // d, d * gw3), BF16) for d in dils],
        scratch_shapes=[pltpu.VMEM((n_stage, tm, LANES), F32)],
        compiler_params=_params(("parallel",)),
        name="qkv_proj",
    )(x2, mod, gain, w_qkv, *tables)


def _attn_kernel(q_ref, k_ref, kp_ref, kn_ref, v_ref, vp_ref, vn_ref, o_ref, lse_ref, kbuf, vbuf, *, length, rows):
    rad, bq = ATT_RADIUS, ATT_Q_BLOCK
    nk = bq + 2 * rad
    i = pl.program_id(2)
    kbuf[0:rad, :] = kp_ref[0]
    kbuf[rad:rad + rows, :] = k_ref[0]
    kbuf[rad + rows:, :] = kn_ref[0]
    vbuf[0:rad, :] = vp_ref[0]
    vbuf[rad:rad + rows, :] = v_ref[0]
    vbuf[rad + rows:, :] = vn_ref[0]

    lane = lax.broadcasted_iota(jnp.int32, (bq, LANES), 1)
    first = lane < ATT_HEAD_DIM
    head_mask = (jnp.where(first, 1.0, 0.0).astype(BF16), jnp.where(first, 0.0, 1.0).astype(BF16))
    t_idx = lax.broadcasted_iota(jnp.int32, (bq, nk), 0)
    c_idx = lax.broadcasted_iota(jnp.int32, (bq, nk), 1)
    rel = c_idx - rad - t_idx
    for jb in range(rows // bq):
        kpos = i * rows + jb * bq - rad + c_idx
        ok = (jnp.abs(rel) <= rad) & (kpos >= 0) & (kpos < length)
        bias = jnp.where(ok, 0.0, NEG_BIG)
        for hp in range(ATT_GROUP_WIDTH // LANES):
            cols = slice(hp * LANES, (hp + 1) * LANES)
            q2 = q_ref[0, jb * bq:(jb + 1) * bq, cols]
            k2 = kbuf[jb * bq:jb * bq + nk, cols]
            v2 = vbuf[jb * bq:jb * bq + nk, cols]
            outs, lses = [], []
            for half in range(2):
                s = lax.dot_general(q2 * head_mask[half], k2, (((1,), (1,)), ((), ())),
                                    preferred_element_type=F32) + bias
                m = jnp.max(s, axis=1, keepdims=True)
                p = jnp.exp(s - m)
                den = jnp.sum(p, axis=1, keepdims=True)
                outs.append(_dot(p.astype(BF16), v2) / den)
                lses.append(m + jnp.log(den))
            o_ref[0, jb * bq:(jb + 1) * bq, cols] = jnp.where(first, outs[0], outs[1]).astype(BF16)
            lse_ref[0, jb * bq:(jb + 1) * bq, cols] = jnp.where(first, lses[0], lses[1])


def _attention_group(qkv, group):
    _, dil = ATT_GROUPS[group]
    batch, length, _ = qkv.shape
    rows, rad, gw = min(ATT_ROWS, length), ATT_RADIUS, ATT_GROUP_WIDTH
    assert length % rows == 0 and rows % ATT_Q_BLOCK == 0
    nblk = length // rows
    halo_per_blk = rows // rad
    last_halo = length // rad - 1
    view = qkv

    def main(kind):
        return pl.BlockSpec((1, rows, gw), lambda b, p, i: (b, i, 3 * p + kind))

    def prev(kind):
        return pl.BlockSpec((1, rad, gw), lambda b, p, i: (b, jnp.maximum(i * halo_per_blk - 1, 0), 3 * p + kind))

    def nxt(kind):
        return pl.BlockSpec((1, rad, gw), lambda b, p, i: (b, jnp.minimum((i + 1) * halo_per_blk, last_halo), 3 * p + kind))

    out_spec = pl.BlockSpec((1, rows, gw), lambda b, p, i: (b, i, p))
    return pl.pallas_call(
        functools.partial(_attn_kernel, length=length, rows=rows),
        grid=(batch, dil, nblk),
        in_specs=[main(0), main(1), prev(1), nxt(1), main(2), prev(2), nxt(2)],
        out_specs=[out_spec, out_spec],
        out_shape=[jax.ShapeDtypeStruct((batch, length, dil * gw), BF16),
                   jax.ShapeDtypeStruct((batch, length, dil * gw), F32)],
        scratch_shapes=[pltpu.VMEM((rows + 2 * rad, gw), BF16), pltpu.VMEM((rows + 2 * rad, gw), BF16)],
        compiler_params=_params(("parallel", "parallel", "parallel")),
        name=f"attn_g{group}",
    )(view, view, view, view, view, view, view)


def _ffn(x1, mod, g2, wgu_ref, wd_ref, act_ref):
    h = _norm_mod(x1, g2, mod[4:5], mod[3:4]).astype(BF16)
    lo = 0
    while lo < FFN_HIDDEN:
        ck = min(FFN_CHUNK, FFN_HIDDEN - lo)
        a = _dot(h, wgu_ref[:, lo:lo + ck])
        b = _dot(h, wgu_ref[:, FFN_HIDDEN + lo:FFN_HIDDEN + lo + ck])
        act_ref[:, lo:lo + ck] = (a * jax.nn.sigmoid(a) * b).astype(BF16)
        lo += ck
    return x1 + mod[5:6] * _dot(act_ref[...], wd_ref[...])


def _l0_post_kernel(x_ref, o0_ref, o1_ref, o2_ref, l0_ref, l1_ref, l2_ref, mod_ref, g2_ref,
                    wo_ref, wgu_ref, wd_ref, out_ref, act_ref, y_ref, stage_ref):
    mod = mod_ref[0]
    tm = x_ref.shape[0]
    gw = ATT_GROUP_WIDTH
    dils = [d for _, d in ATT_GROUPS]
    slab = 0
    for s in range(gw // LANES):
        vals = []
        for ref, dil in zip((o0_ref, o1_ref, o2_ref, l0_ref, l1_ref, l2_ref), dils + dils):
            if dil == 1:
                vals.append(ref[0, :, s * LANES:(s + 1) * LANES].astype(F32))
                continue
            for ph in range(dil):
                lo = ph * gw + s * LANES
                stage_ref[slab, pl.ds(ph, tm // dil, stride=dil), :] = ref[0, :, lo:lo + LANES].astype(F32)
            vals.append(stage_ref[slab])
            slab += 1
        o0, o1, o2, l0, l1, l2 = vals
        top = jnp.maximum(jnp.maximum(l0, l1), l2)
        e0, e1, e2 = jnp.exp(l0 - top), jnp.exp(l1 - top), jnp.exp(l2 - top)
        y_ref[:, s * LANES:(s + 1) * LANES] = ((e0 * o0 + e1 * o1 + e2 * o2) / (e0 + e1 + e2)).astype(BF16)
    x1 = x_ref[...] + mod[2:3] * _dot(y_ref[...], wo_ref[...])
    out_ref[...] = _ffn(x1, mod, g2_ref[...], wgu_ref, wd_ref, act_ref)


def _l0_post(x2, outs, lses, mod, g2, w_o, w_gu, w_down, seq):
    tokens = x2.shape[0]
    tm = TOKEN_TILE
    per_row = seq // tm
    gw = ATT_GROUP_WIDTH
    dils = [d for _, d in ATT_GROUPS]
    tile = lambda w: pl.BlockSpec((tm, w), lambda i: (i, 0))
    att_specs = [pl.BlockSpec((1, tm // d, d * gw), lambda i: (i // per_row, i % per_row, 0)) for d in dils]
    n_stage = 2 * (gw // LANES) * sum(1 for d in dils if d > 1)
    return pl.pallas_call(
        _l0_post_kernel,
        grid=(tokens // tm,),
        in_specs=[tile(D_MODEL)] + att_specs + att_specs + [
            pl.BlockSpec((1, 6, D_MODEL), lambda i: (i // per_row, 0, 0)),
            _const_spec((1, D_MODEL)),
            _const_spec((gw, D_MODEL)),
            _const_spec((D_MODEL, 2 * FFN_HIDDEN)),
            _const_spec((FFN_HIDDEN, D_MODEL)),
        ],
        out_specs=tile(D_MODEL),
        out_shape=jax.ShapeDtypeStruct((tokens, D_MODEL), F32),
        scratch_shapes=[pltpu.VMEM((tm, FFN_HIDDEN), BF16), pltpu.VMEM((tm, gw), BF16),
                        pltpu.VMEM((n_stage, tm, LANES), F32)],
        compiler_params=_params(("parallel",)),
        name="l0_post",
    )(x2, *outs, *lses, mod, g2, w_o, w_gu, w_down)


def _mlstm_in_kernel(x_ref, mod_ref, g_ref, w_ref, wg1_ref, wg2_ref, bg_ref,
                     q_ref, k_ref, v_ref, og_ref, gcol_ref, grow_ref):
    mod = mod_ref[0]
    h32 = _norm_mod(x_ref[...], g_ref[...], mod[1:2], mod[0:1])
    h = h32.astype(BF16)
    h_lo = (h32 - h.astype(F32)).astype(BF16)
    qw, vw = MLSTM_QK_WIDTH, MLSTM_V_WIDTH
    q_ref[...] = _dot(h, w_ref[:, 0:qw]).astype(BF16)
    k_ref[...] = (_dot(h, w_ref[:, qw:2 * qw]) * (MLSTM_QK_DIM ** -0.5)).astype(BF16)
    v_ref[...] = _dot(h, w_ref[:, 2 * qw:2 * qw + vw]).astype(BF16)
    og_ref[...] = jax.nn.sigmoid(_dot(h, w_ref[:, 2 * qw + vw:2 * qw + 2 * vw])).astype(BF16)
    g = _dot(h, wg1_ref[...]) + _dot(h_lo, wg1_ref[...]) + _dot(h, wg2_ref[...]) + bg_ref[...]
    lane = lax.broadcasted_iota(jnp.int32, g.shape, 1)
    is_forget = (((lane >> 2) & 1) == 1) & (lane < MLSTM_GATES)
    log_sig = jnp.minimum(g, 0.0) - jnp.log(1.0 + jnp.exp(-jnp.abs(g)))
    g = jnp.where(is_forget, log_sig, g)
    cl = SCAN_CHUNK
    row = lax.broadcasted_iota(jnp.int32, (cl, cl), 0)
    col = lax.broadcasted_iota(jnp.int32, (cl, cl), 1)
    tri = jnp.where(col <= row, 1.0, 0.0).astype(BF16)
    lane_c = lax.broadcasted_iota(jnp.int32, (cl, LANES), 1)
    fwd_forget = (lane_c >= MLSTM_HEADS) & (lane_c < 2 * MLSTM_HEADS)
    bwd_forget = (lane_c >= 3 * MLSTM_HEADS) & (lane_c < 4 * MLSTM_HEADS)
    for c in range(g.shape[0] // cl):
        gc = g[c * cl:(c + 1) * cl]
        g1, g2, g3 = _split3(gc)
        prefix = _dot(tri, g1) + _dot(tri, g2) + _dot(tri, g3)
        suffix = prefix[cl - 1:cl, :] - prefix + gc
        prepared = jnp.where(fwd_forget, prefix, jnp.where(bwd_forget, suffix, gc))
        gcol_ref[c * cl:(c + 1) * cl, :] = prepared
        grow_ref[c] = prepared.T[0:MLSTM_GATES, :]


def _mlstm_in(x2, mod, gain, w_main, wg1, wg2, bg, seq):
    tokens = x2.shape[0]
    tm = TOKEN_TILE
    per_row = seq // tm
    tile = lambda w: pl.BlockSpec((tm, w), lambda i: (i, 0))
    qw, vw = MLSTM_QK_WIDTH, MLSTM_V_WIDTH
    cpt = tm // SCAN_CHUNK
    return pl.pallas_call(
        _mlstm_in_kernel,
        grid=(tokens // tm,),
        in_specs=[
            tile(D_MODEL),
            pl.BlockSpec((1, 6, D_MODEL), lambda i: (i // per_row, 0, 0)),
            _const_spec((1, D_MODEL)),
            _const_spec((D_MODEL, MLSTM_MAIN_WIDTH)),
            _const_spec((D_MODEL, LANES)),
            _const_spec((D_MODEL, LANES)),
            _const_spec((1, LANES)),
        ],
        out_specs=[tile(qw), tile(qw), tile(vw), tile(vw), tile(LANES),
                   pl.BlockSpec((cpt, MLSTM_GATES, SCAN_CHUNK), lambda i: (i, 0, 0))],
        out_shape=[
            jax.ShapeDtypeStruct((tokens, qw), BF16),
            jax.ShapeDtypeStruct((tokens, qw), BF16),
            jax.ShapeDtypeStruct((tokens, vw), BF16),
            jax.ShapeDtypeStruct((tokens, vw), BF16),
            jax.ShapeDtypeStruct((tokens, LANES), F32),
            jax.ShapeDtypeStruct((tokens // SCAN_CHUNK, MLSTM_GATES, SCAN_CHUNK), F32),
        ],
        compiler_params=_params(("parallel",)),
        name="mlstm_in",
    )(x2, mod, gain, w_main, wg1, wg2, bg)


def _scan_block(q_ref, k_ref, v_ref, gcol_ref, grow_ref, c_ref, n_ref, m_ref, reverse, emit):
    cl, dk, dv = SCAN_CHUNK, MLSTM_QK_DIM, MLSTM_V_DIM
    nc = q_ref.shape[0] // cl
    direction = 1 if reverse else 0
    row = lax.broadcasted_iota(jnp.int32, (cl, cl), 0)
    col = lax.broadcasted_iota(jnp.int32, (cl, cl), 1)
    keep = (col >= row) if reverse else (col <= row)
    last = 0 if reverse else cl - 1
    items = [(h, c) for h in range(MLSTM_HEADS) for c in range(nc)]
    rows = lambda c: slice(c * cl, (c + 1) * cl)
    qs = {(h, c): q_ref[rows(c), h * dk:(h + 1) * dk] for h, c in items}
    ks = {(h, c): k_ref[rows(c), h * dk:(h + 1) * dk] for h, c in items}
    vs = {(h, c): v_ref[rows(c), h * dv:(h + 1) * dv] for h, c in items}

    b_col, i_col, rmax, pexp = {}, {}, {}, {}
    for h, c in items:
        ci = 8 * direction + h
        cf = ci + MLSTM_HEADS
        gc, gr = gcol_ref[rows(c), :], grow_ref[c]
        b_col[h, c], i_col[h, c] = gc[:, cf:cf + 1], gc[:, ci:ci + 1]
        dmat = jnp.where(keep, b_col[h, c] - gr[cf:cf + 1, :] + gr[ci:ci + 1, :], NEG_BIG)
        rmax[h, c] = jnp.max(dmat, axis=1, keepdims=True)
        pexp[h, c] = jnp.exp(dmat - rmax[h, c])
    p_bf, den_loc = {}, {}
    for h, c in items:
        s = lax.dot_general(qs[h, c], ks[h, c], (((1,), (1,)), ((), ())), preferred_element_type=F32)
        p = s * pexp[h, c]
        den_loc[h, c] = jnp.sum(p, axis=1, keepdims=True)
        p_bf[h, c] = p.astype(BF16)
    num_loc = {(h, c): _dot(p_bf[h, c], vs[h, c]) for h, c in items}
    b_last, g_max, kw = {}, {}, {}
    for h, c in items:
        cf = 8 * direction + h + MLSTM_HEADS
        b_last[h, c] = gcol_ref[c * cl + last:c * cl + last + 1, :][:, cf:cf + 1]
        g_col = b_last[h, c] - b_col[h, c] + i_col[h, c]
        g_max[h, c] = jnp.max(g_col, axis=0, keepdims=True)
        kw[h, c] = ks[h, c].astype(F32) * jnp.exp(g_col - g_max[h, c])
    dn = {(h, c): jnp.sum(kw[h, c], axis=0, keepdims=True) for h, c in items}
    dc = {(h, c): _dot(kw[h, c].T.astype(BF16), vs[h, c]) for h, c in items}

    for h in range(MLSTM_HEADS):
        c_state, n_state, m_state = c_ref[h], n_ref[h][0:1, :], m_ref[h][0:1, 0:1]
        for c in (reversed(range(nc)) if reverse else range(nc)):
            m_inter = b_col[h, c] + m_state
            m_t = jnp.maximum(m_inter, rmax[h, c])
            alpha = jnp.exp(rmax[h, c] - m_t)
            inter = jnp.exp(m_inter - m_t)
            q = qs[h, c]
            num = alpha * num_loc[h, c] + inter * _dot(q, c_state.astype(BF16))
            den = alpha * den_loc[h, c] + inter * jnp.sum(q.astype(F32) * n_state, axis=1, keepdims=True)
            emit(h, c, num / jnp.maximum(jnp.abs(den), jnp.exp(-m_t)))
            m_new = jnp.maximum(b_last[h, c] + m_state, g_max[h, c])
            decay = jnp.exp(b_last[h, c] + m_state - m_new)
            beta = jnp.exp(g_max[h, c] - m_new)
            c_state = decay * c_state + beta * dc[h, c]
            n_state = decay * n_state + beta * dn[h, c]
            m_state = m_new
        c_ref[h] = c_state
        n_ref[h] = jnp.broadcast_to(n_state, n_ref.shape[1:])
        m_ref[h] = jnp.broadcast_to(m_state, m_ref.shape[1:])


def _scan_init(c_ref, n_ref, m_ref):
    @pl.when(pl.program_id(1) == 0)
    def _():
        c_ref[...] = jnp.zeros_like(c_ref)
        n_ref[...] = jnp.zeros_like(n_ref)
        m_ref[...] = jnp.zeros_like(m_ref)


def _scan_bwd_kernel(q_ref, k_ref, v_ref, gcol_ref, grow_ref, h_ref, c_ref, n_ref, m_ref):
    _scan_init(c_ref, n_ref, m_ref)
    cl, dv = SCAN_CHUNK, MLSTM_V_DIM

    def emit(h, c, val):
        h_ref[c * cl:(c + 1) * cl, h * dv:(h + 1) * dv] = val

    _scan_block(q_ref, k_ref, v_ref, gcol_ref, grow_ref, c_ref, n_ref, m_ref, True, emit)


def _scan_fwd_kernel(q_ref, k_ref, v_ref, gcol_ref, grow_ref, hb_ref, og_ref, hn_ref, y_ref, c_ref, n_ref, m_ref):
    _scan_init(c_ref, n_ref, m_ref)
    cl, dv = SCAN_CHUNK, MLSTM_V_DIM

    def emit(h, c, val):
        rows, cols = slice(c * cl, (c + 1) * cl), slice(h * dv, (h + 1) * dv)
        hs = val + hb_ref[rows, cols]
        hs = hs * lax.rsqrt(jnp.mean(hs * hs, axis=-1, keepdims=True) + RMS_EPS) * hn_ref[:, cols]
        y_ref[rows, cols] = (og_ref[rows, cols].astype(F32) * hs).astype(BF16)

    _scan_block(q_ref, k_ref, v_ref, gcol_ref, grow_ref, c_ref, n_ref, m_ref, False, emit)


def _scan_scratch():
    return [
        pltpu.VMEM((MLSTM_HEADS, MLSTM_QK_DIM, MLSTM_V_DIM), F32),
        pltpu.VMEM((MLSTM_HEADS, 8, MLSTM_QK_DIM), F32),
        pltpu.VMEM((MLSTM_HEADS, 8, LANES), F32),
    ]


def _mlstm_scans(q, k, v, gcol, grow, og, head_norm, batch, seq):
    rows = SCAN_ROWS
    per_row = seq // rows
    qw, vw = MLSTM_QK_WIDTH, MLSTM_V_WIDTH
    tokens = batch * seq
    cpb = rows // SCAN_CHUNK

    def blk(b, i, reverse):
        return b * per_row + (per_row - 1 - i if reverse else i)

    def spec(width, reverse):
        return pl.BlockSpec((rows, width), lambda b, i: (blk(b, i, reverse), 0))

    def row_spec(reverse):
        return pl.BlockSpec((cpb, MLSTM_GATES, SCAN_CHUNK), lambda b, i: (blk(b, i, reverse), 0, 0))

    params = _params(("parallel", "arbitrary"))
    h_bw = pl.pallas_call(
        _scan_bwd_kernel,
        grid=(batch, per_row),
        in_specs=[spec(qw, True), spec(qw, True), spec(vw, True), spec(LANES, True), row_spec(True)],
        out_specs=spec(vw, True),
        out_shape=jax.ShapeDtypeStruct((tokens, vw), F32),
        scratch_shapes=_scan_scratch(),
        compiler_params=params,
        name="mlstm_bwd",
    )(q, k, v, gcol, grow)
    return pl.pallas_call(
        _scan_fwd_kernel,
        grid=(batch, per_row),
        in_specs=[spec(qw, False), spec(qw, False), spec(vw, False), spec(LANES, False), row_spec(False),
                  spec(vw, False), spec(vw, False), _const_spec((1, vw))],
        out_specs=spec(vw, False),
        out_shape=jax.ShapeDtypeStruct((tokens, vw), BF16),
        scratch_shapes=_scan_scratch(),
        compiler_params=params,
        name="mlstm_fwd",
    )(q, k, v, gcol, grow, h_bw, og, head_norm)


def _l1_post_kernel(x_ref, y_ref, mod_ref, g2_ref, gf_ref, wout_ref, wgu_ref, wd_ref, out_ref, act_ref):
    mod = mod_ref[0]
    x1 = x_ref[...] + mod[2:3] * _dot(y_ref[...], wout_ref[...])
    x2 = _ffn(x1, mod, g2_ref[...], wgu_ref, wd_ref, act_ref)
    out_ref[...] = x2 * lax.rsqrt(jnp.mean(x2 * x2, axis=-1, keepdims=True) + RMS_EPS) * gf_ref[...]


def _l1_post(x2, y, mod, g2, g_final, w_out, w_gu, w_down, seq):
    tokens = x2.shape[0]
    tm = TOKEN_TILE
    per_row = seq // tm
    tile = lambda w: pl.BlockSpec((tm, w), lambda i: (i, 0))
    return pl.pallas_call(
        _l1_post_kernel,
        grid=(tokens // tm,),
        in_specs=[
            tile(D_MODEL), tile(MLSTM_V_WIDTH),
            pl.BlockSpec((1, 6, D_MODEL), lambda i: (i // per_row, 0, 0)),
            _const_spec((1, D_MODEL)),
            _const_spec((1, D_MODEL)),
            _const_spec((MLSTM_V_WIDTH, D_MODEL)),
            _const_spec((D_MODEL, 2 * FFN_HIDDEN)),
            _const_spec((FFN_HIDDEN, D_MODEL)),
        ],
        out_specs=tile(D_MODEL),
        out_shape=jax.ShapeDtypeStruct((tokens, D_MODEL), F32),
        scratch_shapes=[pltpu.VMEM((tm, FFN_HIDDEN), BF16)],
        compiler_params=_params(("parallel",)),
        name="l1_post",
    )(x2, y, mod, g2, g_final, w_out, w_gu, w_down)


def _trunk(x, mod0, mod1, weights, tables):
    batch, seq, _ = x.shape
    assert seq % TOKEN_TILE == 0 and seq % SCAN_ROWS == 0
    x2 = x.reshape(batch * seq, D_MODEL)
    qkv = _qkv_proj(x2, mod0, weights["l0_norm1"], weights["w_qkv"], tables, batch, seq)
    outs, lses = zip(*[_attention_group(qkv[g], g) for g in range(len(ATT_GROUPS))])
    x2 = _l0_post(x2, outs, lses, mod0, weights["l0_norm2"], weights["w_o"], weights["l0_w_gu"],
                  weights["l0_w_down"], seq)
    q, k, v, og, gcol, grow = _mlstm_in(x2, mod1, weights["l1_norm1"], weights["w_in"], weights["wg1"],
                                        weights["wg2"], weights["b_gates"], seq)
    y = _mlstm_scans(q, k, v, gcol, grow, og, weights["head_norm"], batch, seq)
    out = _l1_post(x2, y, mod1, weights["l1_norm2"], weights["final_norm"], weights["w_out"],
                   weights["l1_w_gu"], weights["l1_w_down"], seq)
    return out.reshape(batch, seq, D_MODEL)


def kernel(x_prompt, x_sample, c_prompt, c_sample, l0_ada_w, l0_ada_b, l0_norm1, l0_attn_w_qkv, l0_attn_w_o, l0_norm2, l0_ffn_w_gu, l0_ffn_w_down, l1_ada_w, l1_ada_b, l1_norm1, l1_mlstm_w_in, l1_mlstm_b_gates, l1_mlstm_head_norm, l1_mlstm_w_out, l1_norm2, l1_ffn_w_gu, l1_ffn_w_down, final_norm):
    row = lambda g: g.reshape(1, -1).astype(F32)
    w_gate = jnp.pad(l1_mlstm_w_in[:, MLSTM_MAIN_WIDTH:], ((0, 0), (0, LANES - MLSTM_GATES)))
    wg1 = w_gate.astype(BF16)
    weights = {
        "l0_norm1": row(l0_norm1), "l0_norm2": row(l0_norm2), "l1_norm1": row(l1_norm1), "l1_norm2": row(l1_norm2),
        "final_norm": row(final_norm), "head_norm": row(l1_mlstm_head_norm),
        "w_qkv": l0_attn_w_qkv.astype(BF16), "w_o": l0_attn_w_o.astype(BF16),
        "l0_w_gu": l0_ffn_w_gu.astype(BF16), "l0_w_down": l0_ffn_w_down.astype(BF16),
        "w_in": l1_mlstm_w_in[:, :MLSTM_MAIN_WIDTH].astype(BF16),
        "wg1": wg1, "wg2": (w_gate - wg1.astype(F32)).astype(BF16),
        "b_gates": jnp.pad(l1_mlstm_b_gates, (0, LANES - MLSTM_GATES)).reshape(1, LANES).astype(F32),
        "w_out": l1_mlstm_w_out.astype(BF16),
        "l1_w_gu": l1_ffn_w_gu.astype(BF16), "l1_w_down": l1_ffn_w_down.astype(BF16),
    }
    nb = x_prompt.shape[0]
    c_all = jnp.concatenate([c_prompt, c_sample], axis=0)
    mod0 = _ada_mod(c_all, l0_ada_w, l0_ada_b)
    mod1 = _ada_mod(c_all, l1_ada_w, l1_ada_b)
    tables = _rope_tables(max(x_prompt.shape[1], x_sample.shape[1]))
    y_prompt = _trunk(x_prompt, mod0[:nb], mod1[:nb], weights, tables)
    y_sample = _trunk(x_sample, mod0[nb:], mod1[nb:], weights, tables)
    return (y_prompt, y_sample)
```

```python
import functools
import math

import jax
import jax.numpy as jnp
from jax import lax
from jax.experimental import pallas as pl
from jax.experimental.pallas import tpu as pltpu

F32 = jnp.float32
BF16 = jnp.bfloat16

D_MODEL = 1024
ATT_GROUPS = ((128, 1), (512, 4), (2048, 16))
ATT_HEADS = 8
ATT_HEAD_DIM = 64
ATT_GROUP_WIDTH = ATT_HEADS * ATT_HEAD_DIM
ATT_QKV_WIDTH = len(ATT_GROUPS) * 3 * ATT_GROUP_WIDTH
ATT_RADIUS = ATT_GROUPS[0][0] // (2 * ATT_GROUPS[0][1])
assert all(w // (2 * d) == ATT_RADIUS for w, d in ATT_GROUPS)
ROPE_DIM = ATT_HEAD_DIM // 4
ROPE_HALF = ROPE_DIM // 2
ROPE_THETA = 500000.0
MLSTM_HEADS = 4
MLSTM_QK_DIM = D_MODEL // (2 * MLSTM_HEADS)
MLSTM_V_DIM = D_MODEL // MLSTM_HEADS
MLSTM_QK_WIDTH = MLSTM_HEADS * MLSTM_QK_DIM
MLSTM_V_WIDTH = MLSTM_HEADS * MLSTM_V_DIM
MLSTM_MAIN_WIDTH = 2 * MLSTM_QK_WIDTH + 2 * MLSTM_V_WIDTH
MLSTM_GATES = 4 * MLSTM_HEADS
MLSTM_VT_ROWS = MLSTM_V_DIM + 16
FFN_HIDDEN = int(math.ceil(8 * D_MODEL / 3 / 256)) * 256
RMS_EPS = 1e-6

LANES = 128
TOKEN_TILE = 512
ATT_Q_BLOCK = 128
ATT_ROWS = 256
SCAN_CHUNK = 128
SCAN_ROWS = 512
FFN_CHUNK = 512
NEG_BIG = -1e30
VMEM_LIMIT = 56 * 1024 * 1024


def _params(semantics):
    return pltpu.CompilerParams(dimension_semantics=semantics, vmem_limit_bytes=VMEM_LIMIT)


def _const_spec(shape):
    zeros = (0,) * len(shape)
    return pl.BlockSpec(shape, lambda *_: zeros, pipeline_mode=pl.Buffered(1))


def _split3(x):
    a = x.astype(BF16)
    r = x - a.astype(F32)
    b = r.astype(BF16)
    c = (r - b.astype(F32)).astype(BF16)
    return a, b, c


def _dot(a, b):
    return jnp.dot(a, b, preferred_element_type=F32)


def _norm_mod(x, gain, scale, shift):
    y = x * lax.rsqrt(jnp.mean(x * x, axis=-1, keepdims=True) + RMS_EPS)
    return (y * gain) * (1.0 + scale) + shift


def _ada_kernel(c_ref, w_ref, b_ref, o_ref):
    c = c_ref[...]
    a = c * jax.nn.sigmoid(c)
    a1, a2, _ = _split3(a)
    w1, w2, _ = _split3(w_ref[...])
    o_ref[...] = _dot(a1, w1) + _dot(a2, w1) + _dot(a1, w2) + b_ref[...]


def _ada_mod(c, ada_w, ada_b):
    nb = c.shape[0]
    width = ada_w.shape[1]
    tile = width // 4
    out = pl.pallas_call(
        _ada_kernel,
        grid=(4,),
        in_specs=[
            pl.BlockSpec((nb, D_MODEL), lambda j: (0, 0)),
            pl.BlockSpec((D_MODEL, tile), lambda j: (0, j)),
            pl.BlockSpec((1, tile), lambda j: (0, j)),
        ],
        out_specs=pl.BlockSpec((nb, tile), lambda j: (0, j)),
        out_shape=jax.ShapeDtypeStruct((nb, width), F32),
        compiler_params=_params(("arbitrary",)),
        name="ada_mod",
    )(c, ada_w, ada_b.reshape(1, width))
    return out.reshape(nb, 6, D_MODEL)


def _rope_tables(seq):
    inv = jnp.float32(ROPE_THETA) ** (-jnp.arange(ROPE_HALF, dtype=F32) / ROPE_HALF)
    ang = jnp.arange(seq, dtype=F32)[:, None] * inv[None, :]
    cos, sin = jnp.cos(ang), jnp.sin(ang)
    rest = ATT_HEAD_DIM - ROPE_DIM
    zeros = jnp.zeros((seq, ROPE_HALF), F32)
    c = jnp.concatenate([cos, cos, jnp.ones((seq, rest), F32)], axis=1)
    s_up = jnp.concatenate([-sin, zeros, jnp.zeros((seq, rest), F32)], axis=1)
    s_dn = jnp.concatenate([zeros, sin, jnp.zeros((seq, rest), F32)], axis=1)
    reps = LANES // ATT_HEAD_DIM
    return tuple(jnp.tile(t, (1, reps)) for t in (c, s_up, s_dn))


def _qkv_kernel(x_ref, mod_ref, g_ref, w_ref, c_ref, su_ref, sd_ref, o0_ref, o1_ref, o2_ref, stage_ref):
    mod = mod_ref[0]
    h = _norm_mod(x_ref[...], g_ref[...], mod[1:2], mod[0:1]).astype(BF16)
    cos, s_up, s_dn = c_ref[...], su_ref[...], sd_ref[...]
    gw = ATT_GROUP_WIDTH
    tm = x_ref.shape[0]
    slab = 0
    for blk in range(ATT_QKV_WIDTH // gw):
        grp, kind = divmod(blk, 3)
        dil = ATT_GROUPS[grp][1]
        o_ref = (o0_ref, o1_ref, o2_ref)[grp]
        p = _dot(h, w_ref[:, blk * gw:(blk + 1) * gw])
        for s in range(gw // LANES):
            val = p[:, s * LANES:(s + 1) * LANES]
            if kind < 2:
                val = val * cos + pltpu.roll(val, LANES - ROPE_HALF, 1) * s_up + pltpu.roll(val, ROPE_HALF, 1) * s_dn
            if kind == 0:
                val = val * (ATT_HEAD_DIM ** -0.5)
            lo = kind * gw + s * LANES
            if dil == 1:
                o_ref[0, :, lo:lo + LANES] = val.astype(BF16)
                continue
            stage_ref[slab] = val
            for ph in range(dil):
                rows = stage_ref[slab, pl.ds(ph, tm // dil, stride=dil), :]
                o_ref[0, :, ph * 3 * gw + lo:ph * 3 * gw + lo + LANES] = rows.astype(BF16)
            slab += 1


def _qkv_proj(x2, mod, gain, w_qkv, tables, batch, seq):
    tokens = x2.shape[0]
    tm = TOKEN_TILE
    per_row = seq // tm
    tab_spec = pl.BlockSpec((tm, LANES), lambda i: (i % per_row, 0))
    gw3 = 3 * ATT_GROUP_WIDTH
    dils = [d for _, d in ATT_GROUPS]
    n_stage = sum(gw3 // LANES for d in dils if d > 1)
    return pl.pallas_call(
        _qkv_kernel,
        grid=(tokens // tm,),
        in_specs=[
            pl.BlockSpec((tm, D_MODEL), lambda i: (i, 0)),
            pl.BlockSpec((1, 6, D_MODEL), lambda i: (i // per_row, 0, 0)),
            _const_spec((1, D_MODEL)),
            _const_spec((D_MODEL, ATT_QKV_WIDTH)),
            tab_spec, tab_spec, tab_spec,
        ],
        out_specs=[pl.BlockSpec((1, tm // d, d * gw3), lambda i: (i // per_row, i % per_row, 0)) for d in dils],
        out_shape=[jax.ShapeDtypeStruct((batch, seq // d, d * gw3), BF16) for d in dils],
        scratch_shapes=[pltpu.VMEM((n_stage, tm, LANES), F32)],
        compiler_params=_params(("parallel",)),
        name="qkv_proj",
    )(x2, mod, gain, w_qkv, *tables)


def _attn_kernel(q_ref, k_ref, kp_ref, kn_ref, v_ref, vp_ref, vn_ref, o_ref, lse_ref, kbuf, vbuf, *, length, rows):
    rad, bq = ATT_RADIUS, ATT_Q_BLOCK
    nk = bq + 2 * rad
    i = pl.program_id(2)
    kbuf[0:rad, :] = kp_ref[0]
    kbuf[rad:rad + rows, :] = k_ref[0]
    kbuf[rad + rows:, :] = kn_ref[0]
    vbuf[0:rad, :] = vp_ref[0]
    vbuf[rad:rad + rows, :] = v_ref[0]
    vbuf[rad + rows:, :] = vn_ref[0]

    lane = lax.broadcasted_iota(jnp.int32, (bq, LANES), 1)
    first = lane < ATT_HEAD_DIM
    head_mask = (jnp.where(first, 1.0, 0.0).astype(BF16), jnp.where(first, 0.0, 1.0).astype(BF16))
    t_idx = lax.broadcasted_iota(jnp.int32, (bq, nk), 0)
    c_idx = lax.broadcasted_iota(jnp.int32, (bq, nk), 1)
    rel = c_idx - rad - t_idx
    for jb in range(rows // bq):
        kpos = i * rows + jb * bq - rad + c_idx
        ok = (jnp.abs(rel) <= rad) & (kpos >= 0) & (kpos < length)
        bias = jnp.where(ok, 0.0, NEG_BIG)
        for hp in range(ATT_GROUP_WIDTH // LANES):
            cols = slice(hp * LANES, (hp + 1) * LANES)
            q2 = q_ref[0, jb * bq:(jb + 1) * bq, cols]
            k2 = kbuf[jb * bq:jb * bq + nk, cols]
            v2 = vbuf[jb * bq:jb * bq + nk, cols]
            outs, lses = [], []
            for half in range(2):
                s = lax.dot_general(q2 * head_mask[half], k2, (((1,), (1,)), ((), ())),
                                    preferred_element_type=F32) + bias
                m = jnp.max(s, axis=1, keepdims=True)
                p = jnp.exp(s - m)
                den = jnp.sum(p, axis=1, keepdims=True)
                outs.append(_dot(p.astype(BF16), v2) / den)
                lses.append(m + jnp.log(den))
            o_ref[0, jb * bq:(jb + 1) * bq, cols] = jnp.where(first, outs[0], outs[1]).astype(BF16)
            lse_ref[0, jb * bq:(jb + 1) * bq, cols] = jnp.where(first, lses[0], lses[1])


def _attention_group(qkv, group):
    _, dil = ATT_GROUPS[group]
    batch, length, _ = qkv.shape
    rows, rad, gw = min(ATT_ROWS, length), ATT_RADIUS, ATT_GROUP_WIDTH
    assert length % rows == 0 and rows % ATT_Q_BLOCK == 0
    nblk = length // rows
    halo_per_blk = rows // rad
    last_halo = length // rad - 1
    view = qkv

    def main(kind):
        return pl.BlockSpec((1, rows, gw), lambda b, p, i: (b, i, 3 * p + kind))

    def prev(kind):
        return pl.BlockSpec((1, rad, gw), lambda b, p, i: (b, jnp.maximum(i * halo_per_blk - 1, 0), 3 * p + kind))

    def nxt(kind):
        return pl.BlockSpec((1, rad, gw), lambda b, p, i: (b, jnp.minimum((i + 1) * halo_per_blk, last_halo), 3 * p + kind))

    out_spec = pl.BlockSpec((1, rows, gw), lambda b, p, i: (b, i, p))
    return pl.pallas_call(
        functools.partial(_attn_kernel, length=length, rows=rows),
        grid=(batch, dil, nblk),
        in_specs=[main(0), main(1), prev(1), nxt(1), main(2), prev(2), nxt(2)],
        out_specs=[out_spec, out_spec],
        out_shape=[jax.ShapeDtypeStruct((batch, length, dil * gw), BF16),
                   jax.ShapeDtypeStruct((batch, length, dil * gw), F32)],
        scratch_shapes=[pltpu.VMEM((rows + 2 * rad, gw), BF16), pltpu.VMEM((rows + 2 * rad, gw), BF16)],
        compiler_params=_params(("parallel", "parallel", "parallel")),
        name=f"attn_g{group}",
    )(view, view, view, view, view, view, view)


def _ffn(x1, mod, g2, wgu_ref, wd_ref, act_ref):
    h = _norm_mod(x1, g2, mod[4:5], mod[3:4]).astype(BF16)
    lo = 0
    while lo < FFN_HIDDEN:
        ck = min(FFN_CHUNK, FFN_HIDDEN - lo)
        a = _dot(h, wgu_ref[:, lo:lo + ck])
        b = _dot(h, wgu_ref[:, FFN_HIDDEN + lo:FFN_HIDDEN + lo + ck])
        act_ref[:, lo:lo + ck] = (a * jax.nn.sigmoid(a) * b).astype(BF16)
        lo += ck
    return x1 + mod[5:6] * _dot(act_ref[...], wd_ref[...])


def _l0_post_kernel(x_ref, o0_ref, o1_ref, o2_ref, l0_ref, l1_ref, l2_ref, mod_ref, g2_ref,
                    wo_ref, wgu_ref, wd_ref, out_ref, act_ref, y_ref, stage_ref):
    mod = mod_ref[0]
    tm = x_ref.shape[0]
    gw = ATT_GROUP_WIDTH
    dils = [d for _, d in ATT_GROUPS]
    slab = 0
    for s in range(gw // LANES):
        vals = []
        for ref, dil in zip((o0_ref, o1_ref, o2_ref, l0_ref, l1_ref, l2_ref), dils + dils):
            if dil == 1:
                vals.append(ref[0, :, s * LANES:(s + 1) * LANES].astype(F32))
                continue
            for ph in range(dil):
                lo = ph * gw + s * LANES
                stage_ref[slab, pl.ds(ph, tm // dil, stride=dil), :] = ref[0, :, lo:lo + LANES].astype(F32)
            vals.append(stage_ref[slab])
            slab += 1
        o0, o1, o2, l0, l1, l2 = vals
        top = jnp.maximum(jnp.maximum(l0, l1), l2)
        e0, e1, e2 = jnp.exp(l0 - top), jnp.exp(l1 - top), jnp.exp(l2 - top)
        y_ref[:, s * LANES:(s + 1) * LANES] = ((e0 * o0 + e1 * o1 + e2 * o2) / (e0 + e1 + e2)).astype(BF16)
    x1 = x_ref[...] + mod[2:3] * _dot(y_ref[...], wo_ref[...])
    out_ref[...] = _ffn(x1, mod, g2_ref[...], wgu_ref, wd_ref, act_ref)


def _l0_post(x2, outs, lses, mod, g2, w_o, w_gu, w_down, seq):
    tokens = x2.shape[0]
    tm = TOKEN_TILE
    per_row = seq // tm
    gw = ATT_GROUP_WIDTH
    dils = [d for _, d in ATT_GROUPS]
    tile = lambda w: pl.BlockSpec((tm, w), lambda i: (i, 0))
    att_specs = [pl.BlockSpec((1, tm // d, d * gw), lambda i: (i // per_row, i % per_row, 0)) for d in dils]
    n_stage = 2 * (gw // LANES) * sum(1 for d in dils if d > 1)
    return pl.pallas_call(
        _l0_post_kernel,
        grid=(tokens // tm,),
        in_specs=[tile(D_MODEL)] + att_specs + att_specs + [
            pl.BlockSpec((1, 6, D_MODEL), lambda i: (i // per_row, 0, 0)),
            _const_spec((1, D_MODEL)),
            _const_spec((gw, D_MODEL)),
            _const_spec((D_MODEL, 2 * FFN_HIDDEN)),
            _const_spec((FFN_HIDDEN, D_MODEL)),
        ],
        out_specs=tile(D_MODEL),
        out_shape=jax.ShapeDtypeStruct((tokens, D_MODEL), F32),
        scratch_shapes=[pltpu.VMEM((tm, FFN_HIDDEN), BF16), pltpu.VMEM((tm, gw), BF16),
                        pltpu.VMEM((n_stage, tm, LANES), F32)],
        compiler_params=_params(("parallel",)),
        name="l0_post",
    )(x2, *outs, *lses, mod, g2, w_o, w_gu, w_down)


def _nt_dot(a, b):
    return lax.dot_general(a, b, (((1,), (1,)), ((), ())), preferred_element_type=F32)


def _mlstm_in_kernel(x_ref, mod_ref, g_ref, wqt_ref, wk_ref, wvt_ref, wo_ref, wg_ref, bg_ref,
                     qt_ref, k_ref, vt_ref, og_ref, pexp_ref, kw_ref, grow_ref, gcm_ref):
    mod = mod_ref[0]
    h32 = _norm_mod(x_ref[...], g_ref[...], mod[1:2], mod[0:1])
    h = h32.astype(BF16)
    dk, dv, vr = MLSTM_QK_DIM, MLSTM_V_DIM, MLSTM_VT_ROWS
    tm = x_ref.shape[0]
    ones_rows = jnp.where(lax.broadcasted_iota(jnp.int32, (vr - dv, tm), 0) == 0, 1.0, 0.0).astype(BF16)

    def project(part):
        if part == 0:
            qt_ref[...] = _nt_dot(wqt_ref[...], h).astype(BF16)
        elif part in (1, 2):
            for hd in range(2 * (part - 1), 2 * part):
                vt_ref[hd * vr:hd * vr + dv, :] = _nt_dot(wvt_ref[hd * dv:(hd + 1) * dv, :], h).astype(BF16)
                vt_ref[hd * vr + dv:(hd + 1) * vr, :] = ones_rows
        else:
            og_ref[...] = jax.nn.sigmoid(_dot(h, wo_ref[...])).astype(BF16)

    g2 = _dot(h, wg_ref[...])
    g = g2[:, 0:LANES] + g2[:, LANES:2 * LANES] + bg_ref[...]
    k32 = _dot(h, wk_ref[...]) * (dk ** -0.5)
    k_ref[...] = k32.astype(BF16)
    lane = lax.broadcasted_iota(jnp.int32, g.shape, 1)
    is_forget = (((lane >> 2) & 1) == 1) & (lane < MLSTM_GATES)
    log_sig = jnp.minimum(g, 0.0) - jnp.log(1.0 + jnp.exp(-jnp.abs(g)))
    g = jnp.where(is_forget, log_sig, g)
    cl, nh = SCAN_CHUNK, MLSTM_HEADS
    row = lax.broadcasted_iota(jnp.int32, (cl, cl), 0)
    col = lax.broadcasted_iota(jnp.int32, (cl, cl), 1)
    tri = jnp.where(col <= row, 1.0, 0.0).astype(BF16)
    lane_c = lax.broadcasted_iota(jnp.int32, (cl, LANES), 1)
    fwd_forget = (lane_c >= nh) & (lane_c < 2 * nh)
    bwd_forget = (lane_c >= 3 * nh) & (lane_c < 4 * nh)
    lane_r = lax.broadcasted_iota(jnp.int32, (MLSTM_GATES, cl), 1)
    row_r = lax.broadcasted_iota(jnp.int32, (MLSTM_GATES, cl), 0)
    for c in range(tm // cl):
        rows = slice(c * cl, (c + 1) * cl)
        gc = g[rows]
        g1, g2, g3 = _split3(gc)
        prefix = _dot(tri, g1) + _dot(tri, g2) + _dot(tri, g3)
        suffix = prefix[cl - 1:cl, :] - prefix + gc
        prepared = jnp.where(fwd_forget, prefix, jnp.where(bwd_forget, suffix, gc))
        w_cols = pltpu.roll(prepared, nh, 1) - prepared
        grow_ref[c] = prepared.T[0:MLSTM_GATES, :]
        w_rows = w_cols.T[0:MLSTM_GATES, :]
        run_fwd, run_bwd = w_rows, w_rows
        shift = 1
        while shift < cl:
            run_fwd = jnp.maximum(run_fwd, jnp.where(lane_r >= shift, pltpu.roll(run_fwd, shift, 1), NEG_BIG))
            run_bwd = jnp.maximum(run_bwd, jnp.where(lane_r < cl - shift, pltpu.roll(run_bwd, cl - shift, 1), NEG_BIG))
            shift *= 2
        cm = jnp.where(row_r < 2 * nh, run_fwd, run_bwd)
        gcm_ref[c] = cm
        for direction in range(2):
            keep = (row >= col) if direction else (row <= col)
            last = 0 if direction else cl - 1
            for hd in range(nh):
                cf = 2 * nh * direction + nh + hd
                w_col = w_cols[:, cf:cf + 1]
                pexp_ref[direction, c, hd] = jnp.exp(jnp.where(keep, w_col - cm[cf:cf + 1, :], NEG_BIG)).astype(BF16)
                w_exp = jnp.exp(w_col - cm[cf:cf + 1, last:last + 1])
                kw_ref[direction, rows, hd * dk:(hd + 1) * dk] = (k32[rows, hd * dk:(hd + 1) * dk] * w_exp).astype(BF16)
        project(c)
    assert tm // cl == 4


def _mlstm_in(x2, mod, gain, weights, seq):
    tokens = x2.shape[0]
    tm = TOKEN_TILE
    per_row = seq // tm
    tile = lambda w: pl.BlockSpec((tm, w), lambda i: (i, 0))
    tile_t = lambda r: pl.BlockSpec((r, tm), lambda i: (0, i))
    qw, vw, nh, cl = MLSTM_QK_WIDTH, MLSTM_V_WIDTH, MLSTM_HEADS, SCAN_CHUNK
    cpt = tm // cl
    chunks = tokens // cl
    row_spec = pl.BlockSpec((cpt, MLSTM_GATES, cl), lambda i: (i, 0, 0))
    return pl.pallas_call(
        _mlstm_in_kernel,
        grid=(tokens // tm,),
        in_specs=[
            tile(D_MODEL),
            pl.BlockSpec((1, 6, D_MODEL), lambda i: (i // per_row, 0, 0)),
            _const_spec((1, D_MODEL)),
            _const_spec((qw, D_MODEL)),
            _const_spec((D_MODEL, qw)),
            _const_spec((vw, D_MODEL)),
            _const_spec((D_MODEL, vw)),
            _const_spec((D_MODEL, 2 * LANES)),
            _const_spec((1, LANES)),
        ],
        out_specs=[tile_t(qw), tile(qw), tile_t(nh * MLSTM_VT_ROWS), tile(vw),
                   pl.BlockSpec((2, cpt, nh, cl, cl), lambda i: (0, i, 0, 0, 0)),
                   pl.BlockSpec((2, tm, qw), lambda i: (0, i, 0)),
                   row_spec, row_spec],
        out_shape=[
            jax.ShapeDtypeStruct((qw, tokens), BF16),
            jax.ShapeDtypeStruct((tokens, qw), BF16),
            jax.ShapeDtypeStruct((nh * MLSTM_VT_ROWS, tokens), BF16),
            jax.ShapeDtypeStruct((tokens, vw), BF16),
            jax.ShapeDtypeStruct((2, chunks, nh, cl, cl), BF16),
            jax.ShapeDtypeStruct((2, tokens, qw), BF16),
            jax.ShapeDtypeStruct((chunks, MLSTM_GATES, cl), F32),
            jax.ShapeDtypeStruct((chunks, MLSTM_GATES, cl), F32),
        ],
        compiler_params=_params(("parallel",)),
        name="mlstm_in",
    )(x2, mod, gain, weights["wq_t"], weights["wk"], weights["wv_t"], weights["wo_gate"], weights["wg"],
      weights["b_gates"])


def _scan_block(qt_ref, k_ref, vt_ref, pexp_ref, kw_ref, grow_ref, gcm_ref, ct_ref, m_ref, reverse, emit):
    cl, dk, dv, vr = SCAN_CHUNK, MLSTM_QK_DIM, MLSTM_V_DIM, MLSTM_VT_ROWS
    nc = k_ref.shape[0] // cl
    direction = 1 if reverse else 0
    last = 0 if reverse else cl - 1
    items = [(h, c) for h in range(MLSTM_HEADS) for c in range(nc)]
    span = lambda c: slice(c * cl, (c + 1) * cl)
    qt = {(h, c): qt_ref[h * dk:(h + 1) * dk, span(c)] for h, c in items}
    vt = {(h, c): vt_ref[h * vr:(h + 1) * vr, span(c)] for h, c in items}

    p_t = {}
    for h, c in items:
        s_t = _dot(k_ref[span(c), h * dk:(h + 1) * dk], qt[h, c])
        p_t[h, c] = (s_t * pexp_ref[c, h].astype(F32)).astype(BF16)
    loc = {(h, c): _dot(vt[h, c], p_t[h, c]) for h, c in items}
    dct = {(h, c): _dot(vt[h, c], kw_ref[span(c), h * dk:(h + 1) * dk]) for h, c in items}

    for h in range(MLSTM_HEADS):
        cf = 2 * MLSTM_HEADS * direction + MLSTM_HEADS + h
        state, m_state = ct_ref[h], m_ref[h][0:1, 0:1]
        for c in (reversed(range(nc)) if reverse else range(nc)):
            b = grow_ref[c][cf:cf + 1, :]
            cm = gcm_ref[c][cf:cf + 1, :]
            m_inter = b + m_state
            m_t = jnp.maximum(m_inter, b + cm)
            alpha = jnp.exp(b + cm - m_t)
            inter = jnp.exp(m_inter - m_t)
            carried = _dot(state.astype(BF16), qt[h, c])
            den = alpha * loc[h, c][dv:dv + 1, :] + inter * carried[dv:dv + 1, :]
            scale = 1.0 / jnp.maximum(jnp.abs(den), jnp.exp(-m_t))
            emit(h, c, (alpha * scale) * loc[h, c][0:dv, :] + (inter * scale) * carried[0:dv, :])
            b_last, g_max = b[:, last:last + 1], b[:, last:last + 1] + cm[:, last:last + 1]
            m_new = jnp.maximum(b_last + m_state, g_max)
            state = jnp.exp(b_last + m_state - m_new) * state + jnp.exp(g_max - m_new) * dct[h, c]
            m_state = m_new
        ct_ref[h] = state
        m_ref[h] = jnp.broadcast_to(m_state, m_ref.shape[1:])


def _scan_init(ct_ref, m_ref):
    @pl.when(pl.program_id(1) == 0)
    def _():
        ct_ref[...] = jnp.zeros_like(ct_ref)
        m_ref[...] = jnp.zeros_like(m_ref)


def _scan_bwd_kernel(qt_ref, k_ref, vt_ref, pexp_ref, kw_ref, grow_ref, gcm_ref, ht_ref, ct_ref, m_ref):
    _scan_init(ct_ref, m_ref)
    cl, dv = SCAN_CHUNK, MLSTM_V_DIM

    def emit(h, c, val):
        ht_ref[h * dv:(h + 1) * dv, c * cl:(c + 1) * cl] = val

    _scan_block(qt_ref, k_ref, vt_ref, pexp_ref, kw_ref, grow_ref, gcm_ref, ct_ref, m_ref, True, emit)


def _scan_fwd_kernel(qt_ref, k_ref, vt_ref, pexp_ref, kw_ref, grow_ref, gcm_ref, hb_ref, og_ref, hn_ref,
                     y_ref, ct_ref, m_ref):
    _scan_init(ct_ref, m_ref)
    cl, dv = SCAN_CHUNK, MLSTM_V_DIM

    def emit(h, c, val):
        rows, cols = slice(c * cl, (c + 1) * cl), slice(h * dv, (h + 1) * dv)
        hs = val + hb_ref[cols, rows]
        hs = hs * lax.rsqrt(jnp.mean(hs * hs, axis=0, keepdims=True) + RMS_EPS) * hn_ref[cols, :]
        y_ref[rows, cols] = (og_ref[rows, cols].astype(F32) * hs.T).astype(BF16)

    _scan_block(qt_ref, k_ref, vt_ref, pexp_ref, kw_ref, grow_ref, gcm_ref, ct_ref, m_ref, False, emit)


def _mlstm_scans(qt, k, vt, og, pexp, kw, grow, gcm, head_norm_lanes, batch, seq):
    rows = SCAN_ROWS
    per_row = seq // rows
    qw, vw, nh, cl = MLSTM_QK_WIDTH, MLSTM_V_WIDTH, MLSTM_HEADS, SCAN_CHUNK
    tokens = batch * seq
    cpb = rows // cl

    def specs(reverse):
        direction = 1 if reverse else 0
        blk = lambda b, i: b * per_row + (per_row - 1 - i if reverse else i)
        nat = lambda width: pl.BlockSpec((rows, width), lambda b, i: (blk(b, i), 0))
        trn = lambda height: pl.BlockSpec((height, rows), lambda b, i: (0, blk(b, i)))
        gate = pl.BlockSpec((cpb, MLSTM_GATES, cl), lambda b, i: (blk(b, i), 0, 0))
        common = [trn(qw), nat(qw), trn(nh * MLSTM_VT_ROWS),
                  pl.BlockSpec((None, cpb, nh, cl, cl), lambda b, i: (direction, blk(b, i), 0, 0, 0)),
                  pl.BlockSpec((None, rows, qw), lambda b, i: (direction, blk(b, i), 0)),
                  gate, gate]
        return common, nat, trn

    scratch = [pltpu.VMEM((nh, MLSTM_VT_ROWS, MLSTM_QK_DIM), F32),
               pltpu.VMEM((nh, 8, LANES), F32)]
    params = _params(("parallel", "arbitrary"))
    common, nat, trn = specs(True)
    ht_bw = pl.pallas_call(
        _scan_bwd_kernel,
        grid=(batch, per_row),
        in_specs=common,
        out_specs=trn(vw),
        out_shape=jax.ShapeDtypeStruct((vw, tokens), F32),
        scratch_shapes=scratch,
        compiler_params=params,
        name="mlstm_bwd",
    )(qt, k, vt, pexp, kw, grow, gcm)
    common, nat, trn = specs(False)
    return pl.pallas_call(
        _scan_fwd_kernel,
        grid=(batch, per_row),
        in_specs=common + [trn(vw), nat(vw), _const_spec((vw, LANES))],
        out_specs=nat(vw),
        out_shape=jax.ShapeDtypeStruct((tokens, vw), BF16),
        scratch_shapes=scratch,
        compiler_params=params,
        name="mlstm_fwd",
    )(qt, k, vt, pexp, kw, grow, gcm, ht_bw, og, head_norm_lanes)


def _l1_post_kernel(x_ref, y_ref, mod_ref, g2_ref, gf_ref, wout_ref, wgu_ref, wd_ref, out_ref, act_ref):
    mod = mod_ref[0]
    x1 = x_ref[...] + mod[2:3] * _dot(y_ref[...], wout_ref[...])
    x2 = _ffn(x1, mod, g2_ref[...], wgu_ref, wd_ref, act_ref)
    out_ref[...] = x2 * lax.rsqrt(jnp.mean(x2 * x2, axis=-1, keepdims=True) + RMS_EPS) * gf_ref[...]


def _l1_post(x2, y, mod, g2, g_final, w_out, w_gu, w_down, seq):
    tokens = x2.shape[0]
    tm = TOKEN_TILE
    per_row = seq // tm
    tile = lambda w: pl.BlockSpec((tm, w), lambda i: (i, 0))
    return pl.pallas_call(
        _l1_post_kernel,
        grid=(tokens // tm,),
        in_specs=[
            tile(D_MODEL), tile(MLSTM_V_WIDTH),
            pl.BlockSpec((1, 6, D_MODEL), lambda i: (i // per_row, 0, 0)),
            _const_spec((1, D_MODEL)),
            _const_spec((1, D_MODEL)),
            _const_spec((MLSTM_V_WIDTH, D_MODEL)),
            _const_spec((D_MODEL, 2 * FFN_HIDDEN)),
            _const_spec((FFN_HIDDEN, D_MODEL)),
        ],
        out_specs=tile(D_MODEL),
        out_shape=jax.ShapeDtypeStruct((tokens, D_MODEL), F32),
        scratch_shapes=[pltpu.VMEM((tm, FFN_HIDDEN), BF16)],
        compiler_params=_params(("parallel",)),
        name="l1_post",
    )(x2, y, mod, g2, g_final, w_out, w_gu, w_down)


def _trunk(x, mod0, mod1, weights, tables):
    batch, seq, _ = x.shape
    assert seq % TOKEN_TILE == 0 and seq % SCAN_ROWS == 0
    x2 = x.reshape(batch * seq, D_MODEL)
    qkv = _qkv_proj(x2, mod0, weights["l0_norm1"], weights["w_qkv"], tables, batch, seq)
    outs, lses = zip(*[_attention_group(qkv[g], g) for g in range(len(ATT_GROUPS))])
    x2 = _l0_post(x2, outs, lses, mod0, weights["l0_norm2"], weights["w_o"], weights["l0_w_gu"],
                  weights["l0_w_down"], seq)
    scan_inputs = _mlstm_in(x2, mod1, weights["l1_norm1"], weights, seq)
    y = _mlstm_scans(*scan_inputs, weights["head_norm_lanes"], batch, seq)
    out = _l1_post(x2, y, mod1, weights["l1_norm2"], weights["final_norm"], weights["w_out"],
                   weights["l1_w_gu"], weights["l1_w_down"], seq)
    return out.reshape(batch, seq, D_MODEL)


def kernel(x_prompt, x_sample, c_prompt, c_sample, l0_ada_w, l0_ada_b, l0_norm1, l0_attn_w_qkv, l0_attn_w_o, l0_norm2, l0_ffn_w_gu, l0_ffn_w_down, l1_ada_w, l1_ada_b, l1_norm1, l1_mlstm_w_in, l1_mlstm_b_gates, l1_mlstm_head_norm, l1_mlstm_w_out, l1_norm2, l1_ffn_w_gu, l1_ffn_w_down, final_norm):
    row = lambda g: g.reshape(1, -1).astype(F32)
    qw, vw = MLSTM_QK_WIDTH, MLSTM_V_WIDTH
    w_gate =jnp.pad(l1_mlstm_w_in[:, MLSTM_MAIN_WIDTH:], ((0, 0), (0, LANES - MLSTM_GATES)))
    wg1 = w_gate.astype(BF16)
    weights = {
        "l0_norm1": row(l0_norm1), "l0_norm2": row(l0_norm2), "l1_norm1": row(l1_norm1), "l1_norm2": row(l1_norm2),
        "final_norm": row(final_norm),
        "head_norm_lanes": jnp.broadcast_to(l1_mlstm_head_norm.astype(F32)[:, None], (MLSTM_V_WIDTH, LANES)),
        "w_qkv": l0_attn_w_qkv.astype(BF16), "w_o": l0_attn_w_o.astype(BF16),
        "l0_w_gu": l0_ffn_w_gu.astype(BF16), "l0_w_down": l0_ffn_w_down.astype(BF16),
        "wq_t": l1_mlstm_w_in[:, :qw].T.astype(BF16), "wk": l1_mlstm_w_in[:, qw:2 * qw].astype(BF16),
        "wv_t": l1_mlstm_w_in[:, 2 * qw:2 * qw + vw].T.astype(BF16),
        "wo_gate": l1_mlstm_w_in[:, 2 * qw + vw:MLSTM_MAIN_WIDTH].astype(BF16),
        "wg": jnp.concatenate([wg1, (w_gate - wg1.astype(F32)).astype(BF16)], axis=1),
        "b_gates": jnp.pad(l1_mlstm_b_gates, (0, LANES - MLSTM_GATES)).reshape(1, LANES).astype(F32),
        "w_out": l1_mlstm_w_out.astype(BF16),
        "l1_w_gu": l1_ffn_w_gu.astype(BF16), "l1_w_down": l1_ffn_w_down.astype(BF16),
    }
    nb = x_prompt.shape[0]
    c_all = jnp.concatenate([c_prompt, c_sample], axis=0)
    mod0 = _ada_mod(c_all, l0_ada_w, l0_ada_b)
    mod1 = _ada_mod(c_all, l1_ada_w, l1_ada_b)
    tables = _rope_tables(max(x_prompt.shape[1], x_sample.shape[1]))
    y_prompt = _trunk(x_prompt, mod0[:nb], mod1[:nb], weights, tables)
    y_sample = _trunk(x_sample, mod0[nb:], mod1[nb:], weights, tables)
    return (y_prompt, y_sample)
```

```python
import functools
import math

import jax
import jax.numpy as jnp
from jax import lax
from jax.experimental import pallas as pl
from jax.experimental.pallas import tpu as pltpu

F32 = jnp.float32
BF16 = jnp.bfloat16

D_MODEL = 1024
ATT_GROUPS = ((128, 1), (512, 4), (2048, 16))
ATT_HEADS = 8
ATT_HEAD_DIM = 64
ATT_GROUP_WIDTH = ATT_HEADS * ATT_HEAD_DIM
ATT_QKV_WIDTH = len(ATT_GROUPS) * 3 * ATT_GROUP_WIDTH
ATT_RADIUS = ATT_GROUPS[0][0] // (2 * ATT_GROUPS[0][1])
assert all(w // (2 * d) == ATT_RADIUS for w, d in ATT_GROUPS)
ROPE_DIM = ATT_HEAD_DIM // 4
ROPE_HALF = ROPE_DIM // 2
ROPE_THETA = 500000.0
MLSTM_HEADS = 4
MLSTM_QK_DIM = D_MODEL // (2 * MLSTM_HEADS)
MLSTM_V_DIM = D_MODEL // MLSTM_HEADS
MLSTM_QK_WIDTH = MLSTM_HEADS * MLSTM_QK_DIM
MLSTM_V_WIDTH = MLSTM_HEADS * MLSTM_V_DIM
MLSTM_MAIN_WIDTH = 2 * MLSTM_QK_WIDTH + 2 * MLSTM_V_WIDTH
MLSTM_GATES = 4 * MLSTM_HEADS
MLSTM_VT_ROWS = MLSTM_V_DIM + 16
FFN_HIDDEN = int(math.ceil(8 * D_MODEL / 3 / 256)) * 256
RMS_EPS = 1e-6

LANES = 128
TOKEN_TILE = 512
ATT_Q_BLOCK = 128
ATT_ROWS = 256
SCAN_CHUNK = 128
SCAN_ROWS = 512
FFN_CHUNK = 512
NEG_BIG = -1e30
VMEM_LIMIT = 56 * 1024 * 1024


def _params(semantics):
    return pltpu.CompilerParams(dimension_semantics=semantics, vmem_limit_bytes=VMEM_LIMIT)


def _const_spec(shape):
    zeros = (0,) * len(shape)
    return pl.BlockSpec(shape, lambda *_: zeros, pipeline_mode=pl.Buffered(1))


def _shifted(ntiles):
    return (lambda s: jnp.minimum(s, ntiles - 1)), (lambda s: jnp.maximum(s - 1, 0))


def _split3(x):
    a = x.astype(BF16)
    r = x - a.astype(F32)
    b = r.astype(BF16)
    c = (r - b.astype(F32)).astype(BF16)
    return a, b, c


def _dot(a, b):
    return jnp.dot(a, b, preferred_element_type=F32)


def _norm_mod(x, gain, scale, shift):
    y = x * lax.rsqrt(jnp.mean(x * x, axis=-1, keepdims=True) + RMS_EPS)
    return (y * gain) * (1.0 + scale) + shift


def _ada_kernel(c_ref, w_ref, b_ref, o_ref):
    c = c_ref[...]
    a = c * jax.nn.sigmoid(c)
    a1, a2, _ = _split3(a)
    w1, w2, _ = _split3(w_ref[...])
    o_ref[...] = _dot(a1, w1) + _dot(a2, w1) + _dot(a1, w2) + b_ref[...]


def _ada_mod(c, ada_w, ada_b):
    nb = c.shape[0]
    width = ada_w.shape[1]
    tile = width // 4
    out = pl.pallas_call(
        _ada_kernel,
        grid=(4,),
        in_specs=[
            pl.BlockSpec((nb, D_MODEL), lambda j: (0, 0)),
            pl.BlockSpec((D_MODEL, tile), lambda j: (0, j)),
            pl.BlockSpec((1, tile), lambda j: (0, j)),
        ],
        out_specs=pl.BlockSpec((nb, tile), lambda j: (0, j)),
        out_shape=jax.ShapeDtypeStruct((nb, width), F32),
        compiler_params=_params(("arbitrary",)),
        name="ada_mod",
    )(c, ada_w, ada_b.reshape(1, width))
    return out.reshape(nb, 6, D_MODEL)


def _rope_tables(seq):
    inv = jnp.float32(ROPE_THETA) ** (-jnp.arange(ROPE_HALF, dtype=F32) / ROPE_HALF)
    ang = jnp.arange(seq, dtype=F32)[:, None] * inv[None, :]
    cos, sin = jnp.cos(ang), jnp.sin(ang)
    rest = ATT_HEAD_DIM - ROPE_DIM
    zeros = jnp.zeros((seq, ROPE_HALF), F32)
    c = jnp.concatenate([cos, cos, jnp.ones((seq, rest), F32)], axis=1)
    s_up = jnp.concatenate([-sin, zeros, jnp.zeros((seq, rest), F32)], axis=1)
    s_dn = jnp.concatenate([zeros, sin, jnp.zeros((seq, rest), F32)], axis=1)
    reps = LANES // ATT_HEAD_DIM
    return tuple(jnp.tile(t, (1, reps)) for t in (c, s_up, s_dn))


def _phase_major(table, dil, tile):
    if dil == 1:
        return table
    seq, width = table.shape
    return table.reshape(seq // tile, tile // dil, dil, width).transpose(0, 2, 1, 3).reshape(seq, width)


def _qkv_kernel(x_ref, mod_ref, g_ref, w_ref, *rest):
    ngrp = len(ATT_GROUPS)
    tab_refs, o_refs, stage_ref = rest[:3 * ngrp], rest[3 * ngrp:4 * ngrp], rest[4 * ngrp]
    mod = mod_ref[0]
    h32 = _norm_mod(x_ref[...], g_ref[...], mod[1:2], mod[0:1])
    gw = ATT_GROUP_WIDTH
    tm = x_ref.shape[0]
    nslab = D_MODEL // LANES
    for s in range(nslab):
        stage_ref[s] = h32[:, s * LANES:(s + 1) * LANES]
    for grp in range(ngrp):
        dil = ATT_GROUPS[grp][1]
        o_ref = o_refs[grp]
        cos, s_up, s_dn = (t[...] for t in tab_refs[3 * grp:3 * grp + 3])
        if dil == 1:
            h = h32.astype(BF16)
        else:
            h = jnp.concatenate(
                [jnp.concatenate([stage_ref[s, pl.ds(ph, tm // dil, stride=dil), :] for s in range(nslab)], axis=1)
                 for ph in range(dil)], axis=0).astype(BF16)
        for kind in range(3):
            blk = 3 * grp + kind
            p = _dot(h, w_ref[:, blk * gw:(blk + 1) * gw])
            for s in range(gw // LANES):
                val = p[:, s * LANES:(s + 1) * LANES]
                if kind < 2:
                    val = val * cos + pltpu.roll(val, LANES - ROPE_HALF, 1) * s_up + pltpu.roll(val, ROPE_HALF, 1) * s_dn
                if kind == 0:
                    val = val * (ATT_HEAD_DIM ** -0.5)
                val = val.astype(BF16)
                lo = kind * gw + s * LANES
                rows = tm // dil
                for ph in range(dil):
                    o_ref[0, :, ph * 3 * gw + lo:ph * 3 * gw + lo + LANES] = val[ph * rows:(ph + 1) * rows]


def _qkv_proj(x2, mod, gain, w_qkv, tables, batch, seq):
    tokens = x2.shape[0]
    tm = TOKEN_TILE
    per_row = seq // tm
    tab_spec = pl.BlockSpec((tm, LANES), lambda i: (i % per_row, 0))
    gw3 = 3 * ATT_GROUP_WIDTH
    dils = [d for _, d in ATT_GROUPS]
    group_tables = [_phase_major(t[:seq], d, tm) for d in dils for t in tables]
    return pl.pallas_call(
        _qkv_kernel,
        grid=(tokens // tm,),
        in_specs=[
            pl.BlockSpec((tm, D_MODEL), lambda i: (i, 0)),
            pl.BlockSpec((1, 6, D_MODEL), lambda i: (i // per_row, 0, 0)),
            _const_spec((1, D_MODEL)),
            _const_spec((D_MODEL, ATT_QKV_WIDTH)),
        ] + [tab_spec] * len(group_tables),
        out_specs=[pl.BlockSpec((1, tm // d, d * gw3), lambda i: (i // per_row, i % per_row, 0)) for d in dils],
        out_shape=[jax.ShapeDtypeStruct((batch, seq // d, d * gw3), BF16) for d in dils],
        scratch_shapes=[pltpu.VMEM((D_MODEL // LANES, tm, LANES), F32)],
        compiler_params=_params(("parallel",)),
        name="qkv_proj",
    )(x2, mod, gain, w_qkv, *group_tables)


def _attn_kernel(q_ref, k_ref, kp_ref, kn_ref, v_ref, vp_ref, vn_ref, o_ref, lse_ref, kbuf, vbuf, *, length, rows):
    rad, bq = ATT_RADIUS, ATT_Q_BLOCK
    nk = bq + 2 * rad
    i = pl.program_id(2)
    kbuf[0:rad, :] = kp_ref[0]
    kbuf[rad:rad + rows, :] = k_ref[0]
    kbuf[rad + rows:, :] = kn_ref[0]
    vbuf[0:rad, :] = vp_ref[0]
    vbuf[rad:rad + rows, :] = v_ref[0]
    vbuf[rad + rows:, :] = vn_ref[0]

    lane = lax.broadcasted_iota(jnp.int32, (bq, LANES), 1)
    first = lane < ATT_HEAD_DIM
    head_mask = (jnp.where(first, 1.0, 0.0).astype(BF16), jnp.where(first, 0.0, 1.0).astype(BF16))
    t_idx = lax.broadcasted_iota(jnp.int32, (bq, nk), 0)
    c_idx = lax.broadcasted_iota(jnp.int32, (bq, nk), 1)
    rel = c_idx - rad - t_idx
    pairs = [(jb, hp) for jb in range(rows // bq) for hp in range(ATT_GROUP_WIDTH // LANES)]
    items = [(jb, hp, half) for jb, hp in pairs for half in range(2)]
    cols = lambda hp: slice(hp * LANES, (hp + 1) * LANES)
    bias = {}
    for jb in range(rows // bq):
        kpos = i * rows + jb * bq - rad + c_idx
        ok = (jnp.abs(rel) <= rad) & (kpos >= 0) & (kpos < length)
        bias[jb] = jnp.where(ok, 0.0, NEG_BIG)
    scores = {}
    for jb, hp, half in items:
        q2 = q_ref[0, jb * bq:(jb + 1) * bq, cols(hp)] * head_mask[half]
        scores[jb, hp, half] = _nt_dot(q2, kbuf[jb * bq:jb * bq + nk, cols(hp)]) + bias[jb]
    probs, den, lse = {}, {}, {}
    for it in items:
        m = jnp.max(scores[it], axis=1, keepdims=True)
        p = jnp.exp(scores[it] - m)
        den[it] = jnp.sum(p, axis=1, keepdims=True)
        lse[it] = m + jnp.log(den[it])
        probs[it] = p.astype(BF16)
    outs = {(jb, hp, half): _dot(probs[jb, hp, half], vbuf[jb * bq:jb * bq + nk, cols(hp)]) / den[jb, hp, half]
            for jb, hp, half in items}
    for jb, hp in pairs:
        o_ref[0, jb * bq:(jb + 1) * bq, cols(hp)] = jnp.where(first, outs[jb, hp, 0], outs[jb, hp, 1]).astype(BF16)
        lse_ref[0, jb * bq:(jb + 1) * bq, cols(hp)] = jnp.where(first, lse[jb, hp, 0], lse[jb, hp, 1])


def _attention_group(qkv, group):
    _, dil = ATT_GROUPS[group]
    batch, length, _ = qkv.shape
    rows, rad, gw = min(ATT_ROWS, length), ATT_RADIUS, ATT_GROUP_WIDTH
    assert length % rows == 0 and rows % ATT_Q_BLOCK == 0
    nblk = length // rows
    halo_per_blk = rows // rad
    last_halo = length // rad - 1
    view = qkv

    def main(kind):
        return pl.BlockSpec((1, rows, gw), lambda b, p, i: (b, i, 3 * p + kind))

    def prev(kind):
        return pl.BlockSpec((1, rad, gw), lambda b, p, i: (b, jnp.maximum(i * halo_per_blk - 1, 0), 3 * p + kind))

    def nxt(kind):
        return pl.BlockSpec((1, rad, gw), lambda b, p, i: (b, jnp.minimum((i + 1) * halo_per_blk, last_halo), 3 * p + kind))

    out_spec = pl.BlockSpec((1, rows, gw), lambda b, p, i: (b, i, p))
    return pl.pallas_call(
        functools.partial(_attn_kernel, length=length, rows=rows),
        grid=(batch, dil, nblk),
        in_specs=[main(0), main(1), prev(1), nxt(1), main(2), prev(2), nxt(2)],
        out_specs=[out_spec, out_spec],
        out_shape=[jax.ShapeDtypeStruct((batch, length, dil * gw), BF16),
                   jax.ShapeDtypeStruct((batch, length, dil * gw), F32)],
        scratch_shapes=[pltpu.VMEM((rows + 2 * rad, gw), BF16), pltpu.VMEM((rows + 2 * rad, gw), BF16)],
        compiler_params=_params(("parallel", "parallel", "parallel")),
        name=f"attn_g{group}",
    )(view, view, view, view, view, view, view)


def _ffn(x1, mod, g2, wgu_ref, wd_ref, act_ref):
    h = _norm_mod(x1, g2, mod[4:5], mod[3:4]).astype(BF16)
    for lo in range(0, FFN_HIDDEN, FFN_CHUNK):
        hi = min(lo + FFN_CHUNK, FFN_HIDDEN)
        a = _dot(h, wgu_ref[:, lo:hi])
        b = _dot(h, wgu_ref[:, FFN_HIDDEN + lo:FFN_HIDDEN + hi])
        act_ref[:, lo:hi] = (a * jax.nn.sigmoid(a) * b).astype(BF16)
    return x1 + mod[5:6] * _dot(act_ref[...], wd_ref[...])


def _l0_post_kernel(x_ref, o0_ref, o1_ref, o2_ref, l0_ref, l1_ref, l2_ref, mod_ref, g2_ref,
                    wo_ref, wgu_ref, wd_ref, out_ref, act_ref, y_ref, stage_ref):
    mod = mod_ref[0]
    tm = x_ref.shape[0]
    gw = ATT_GROUP_WIDTH
    dils = [d for _, d in ATT_GROUPS]
    slab = 0
    for s in range(gw // LANES):
        vals = []
        for ref, dil in zip((o0_ref, o1_ref, o2_ref, l0_ref, l1_ref, l2_ref), dils + dils):
            if dil == 1:
                vals.append(ref[0, :, s * LANES:(s + 1) * LANES].astype(F32))
                continue
            for ph in range(dil):
                lo = ph * gw + s * LANES
                stage_ref[slab, pl.ds(ph, tm // dil, stride=dil), :] = ref[0, :, lo:lo + LANES].astype(F32)
            vals.append(stage_ref[slab])
            slab += 1
        o0, o1, o2, l0, l1, l2 = vals
        top = jnp.maximum(jnp.maximum(l0, l1), l2)
        e0, e1, e2 = jnp.exp(l0 - top), jnp.exp(l1 - top), jnp.exp(l2 - top)
        y_ref[:, s * LANES:(s + 1) * LANES] = ((e0 * o0 + e1 * o1 + e2 * o2) / (e0 + e1 + e2)).astype(BF16)
    x1 = x_ref[...] + mod[2:3] * _dot(y_ref[...], wo_ref[...])
    out_ref[...] = _ffn(x1, mod, g2_ref[...], wgu_ref, wd_ref, act_ref)


def _l0_post(x2, outs, lses, mod, g2, w_o, w_gu, w_down, seq):
    tokens = x2.shape[0]
    tm = TOKEN_TILE
    per_row = seq // tm
    gw = ATT_GROUP_WIDTH
    dils = [d for _, d in ATT_GROUPS]
    tile = lambda w: pl.BlockSpec((tm, w), lambda i: (i, 0))
    att_specs = [pl.BlockSpec((1, tm // d, d * gw), lambda i: (i // per_row, i % per_row, 0)) for d in dils]
    n_stage = 2 * (gw // LANES) * sum(1 for d in dils if d > 1)
    return pl.pallas_call(
        _l0_post_kernel,
        grid=(tokens // tm,),
        in_specs=[tile(D_MODEL)] + att_specs + att_specs + [
            pl.BlockSpec((1, 6, D_MODEL), lambda i: (i // per_row, 0, 0)),
            _const_spec((1, D_MODEL)),
            _const_spec((gw, D_MODEL)),
            _const_spec((D_MODEL, 2 * FFN_HIDDEN)),
            _const_spec((FFN_HIDDEN, D_MODEL)),
        ],
        out_specs=tile(D_MODEL),
        out_shape=jax.ShapeDtypeStruct((tokens, D_MODEL), F32),
        scratch_shapes=[pltpu.VMEM((tm, FFN_HIDDEN), BF16), pltpu.VMEM((tm, gw), BF16),
                        pltpu.VMEM((n_stage, tm, LANES), F32)],
        compiler_params=_params(("parallel",)),
        name="l0_post",
    )(x2, *outs, *lses, mod, g2, w_o, w_gu, w_down)


def _nt_dot(a, b):
    return lax.dot_general(a, b, (((1,), (1,)), ((), ())), preferred_element_type=F32)


def _mlstm_in_kernel(x_ref, mod_ref, g_ref, wqt_ref, wk_ref, wvt_ref, wo_ref, wg_ref, bg_ref,
                     qt_ref, k_ref, vt_ref, og_ref, pexp_ref, kw_ref, grow_ref, gcm_ref):
    mod = mod_ref[0]
    h32 = _norm_mod(x_ref[...], g_ref[...], mod[1:2], mod[0:1])
    h = h32.astype(BF16)
    dk, dv, vr = MLSTM_QK_DIM, MLSTM_V_DIM, MLSTM_VT_ROWS
    tm = x_ref.shape[0]
    ones_rows = jnp.where(lax.broadcasted_iota(jnp.int32, (vr - dv, tm), 0) == 0, 1.0, 0.0).astype(BF16)

    def project(part):
        if part == 0:
            qt_ref[...] = _nt_dot(wqt_ref[...], h).astype(BF16)
        elif part in (1, 2):
            for hd in range(2 * (part - 1), 2 * part):
                vt_ref[hd * vr:hd * vr + dv, :] = _nt_dot(wvt_ref[hd * dv:(hd + 1) * dv, :], h).astype(BF16)
                vt_ref[hd * vr + dv:(hd + 1) * vr, :] = ones_rows
        else:
            og_ref[...] = jax.nn.sigmoid(_dot(h, wo_ref[...])).astype(BF16)

    g2 = _dot(h, wg_ref[...])
    g = g2[:, 0:LANES] + g2[:, LANES:2 * LANES] + bg_ref[...]
    k32 = _dot(h, wk_ref[...]) * (dk ** -0.5)
    k_ref[...] = k32.astype(BF16)
    lane = lax.broadcasted_iota(jnp.int32, g.shape, 1)
    is_forget = (((lane >> 2) & 1) == 1) & (lane < MLSTM_GATES)
    log_sig = jnp.minimum(g, 0.0) - jnp.log(1.0 + jnp.exp(-jnp.abs(g)))
    g = jnp.where(is_forget, log_sig, g)
    cl, nh = SCAN_CHUNK, MLSTM_HEADS
    row = lax.broadcasted_iota(jnp.int32, (cl, cl), 0)
    col = lax.broadcasted_iota(jnp.int32, (cl, cl), 1)
    tri = jnp.where(col <= row, 1.0, 0.0).astype(BF16)
    lane_c = lax.broadcasted_iota(jnp.int32, (cl, LANES), 1)
    fwd_forget = (lane_c >= nh) & (lane_c < 2 * nh)
    bwd_forget = (lane_c >= 3 * nh) & (lane_c < 4 * nh)
    lane_r = lax.broadcasted_iota(jnp.int32, (MLSTM_GATES, cl), 1)
    row_r = lax.broadcasted_iota(jnp.int32, (MLSTM_GATES, cl), 0)
    for c in range(tm // cl):
        rows = slice(c * cl, (c + 1) * cl)
        gc = g[rows]
        g1, g2, g3 = _split3(gc)
        prefix = _dot(tri, g1) + _dot(tri, g2) + _dot(tri, g3)
        suffix = prefix[cl - 1:cl, :] - prefix + gc
        prepared = jnp.where(fwd_forget, prefix, jnp.where(bwd_forget, suffix, gc))
        w_cols = pltpu.roll(prepared, nh, 1) - prepared
        grow_ref[c] = prepared.T[0:MLSTM_GATES, :]
        w_rows = w_cols.T[0:MLSTM_GATES, :]
        run_fwd, run_bwd = w_rows, w_rows
        shift = 1
        while shift < cl:
            run_fwd = jnp.maximum(run_fwd, jnp.where(lane_r >= shift, pltpu.roll(run_fwd, shift, 1), NEG_BIG))
            run_bwd = jnp.maximum(run_bwd, jnp.where(lane_r < cl - shift, pltpu.roll(run_bwd, cl - shift, 1), NEG_BIG))
            shift *= 2
        cm = jnp.where(row_r < 2 * nh, run_fwd, run_bwd)
        gcm_ref[c] = cm
        for direction in range(2):
            keep = (row >= col) if direction else (row <= col)
            last = 0 if direction else cl - 1
            for hd in range(nh):
                cf = 2 * nh * direction + nh + hd
                w_col = w_cols[:, cf:cf + 1]
                pexp_ref[direction, c, hd] = jnp.exp(jnp.where(keep, w_col - cm[cf:cf + 1, :], NEG_BIG)).astype(BF16)
                w_exp = jnp.exp(w_col - cm[cf:cf + 1, last:last + 1])
                kw_ref[direction, rows, hd * dk:(hd + 1) * dk] = (k32[rows, hd * dk:(hd + 1) * dk] * w_exp).astype(BF16)
    for part in range(4):
        project(part)


def _mlstm_in(x2, mod, gain, weights, seq):
    tokens = x2.shape[0]
    tm = TOKEN_TILE
    per_row = seq // tm
    tile = lambda w: pl.BlockSpec((tm, w), lambda i: (i, 0))
    tile_t = lambda r: pl.BlockSpec((r, tm), lambda i: (0, i))
    qw, vw, nh, cl = MLSTM_QK_WIDTH, MLSTM_V_WIDTH, MLSTM_HEADS, SCAN_CHUNK
    cpt = tm // cl
    chunks = tokens // cl
    row_spec = pl.BlockSpec((cpt, MLSTM_GATES, cl), lambda i: (i, 0, 0))
    return pl.pallas_call(
        _mlstm_in_kernel,
        grid=(tokens // tm,),
        in_specs=[
            tile(D_MODEL),
            pl.BlockSpec((1, 6, D_MODEL), lambda i: (i // per_row, 0, 0)),
            _const_spec((1, D_MODEL)),
            _const_spec((qw, D_MODEL)),
            _const_spec((D_MODEL, qw)),
            _const_spec((vw, D_MODEL)),
            _const_spec((D_MODEL, vw)),
            _const_spec((D_MODEL, 2 * LANES)),
            _const_spec((1, LANES)),
        ],
        out_specs=[tile_t(qw), tile(qw), tile_t(nh * MLSTM_VT_ROWS), tile(vw),
                   pl.BlockSpec((2, cpt, nh, cl, cl), lambda i: (0, i, 0, 0, 0)),
                   pl.BlockSpec((2, tm, qw), lambda i: (0, i, 0)),
                   row_spec, row_spec],
        out_shape=[
            jax.ShapeDtypeStruct((qw, tokens), BF16),
            jax.ShapeDtypeStruct((tokens, qw), BF16),
            jax.ShapeDtypeStruct((nh * MLSTM_VT_ROWS, tokens), BF16),
            jax.ShapeDtypeStruct((tokens, vw), BF16),
            jax.ShapeDtypeStruct((2, chunks, nh, cl, cl), BF16),
            jax.ShapeDtypeStruct((2, tokens, qw), BF16),
            jax.ShapeDtypeStruct((chunks, MLSTM_GATES, cl), F32),
            jax.ShapeDtypeStruct((chunks, MLSTM_GATES, cl), F32),
        ],
        compiler_params=_params(("parallel",)),
        name="mlstm_in",
    )(x2, mod, gain, weights["wq_t"], weights["wk"], weights["wv_t"], weights["wo_gate"], weights["wg"],
      weights["b_gates"])


def _scan_block(qt_ref, k_ref, vt_ref, pexp_ref, kw_ref, grow_ref, gcm_ref, ct_ref, m_ref, reverse, emit):
    cl, dk, dv, vr = SCAN_CHUNK, MLSTM_QK_DIM, MLSTM_V_DIM, MLSTM_VT_ROWS
    nc = k_ref.shape[0] // cl
    direction = 1 if reverse else 0
    last = 0 if reverse else cl - 1
    items = [(h, c) for h in range(MLSTM_HEADS) for c in range(nc)]
    span = lambda c: slice(c * cl, (c + 1) * cl)
    qt = {(h, c): qt_ref[h * dk:(h + 1) * dk, span(c)] for h, c in items}
    vt = {(h, c): vt_ref[h * vr:(h + 1) * vr, span(c)] for h, c in items}

    p_t = {}
    for h, c in items:
        s_t = _dot(k_ref[span(c), h * dk:(h + 1) * dk], qt[h, c])
        p_t[h, c] = (s_t * pexp_ref[c, h].astype(F32)).astype(BF16)
    loc = {(h, c): _dot(vt[h, c], p_t[h, c]) for h, c in items}
    dct = {(h, c): _dot(vt[h, c], kw_ref[span(c), h * dk:(h + 1) * dk]) for h, c in items}

    for h in range(MLSTM_HEADS):
        cf = 2 * MLSTM_HEADS * direction + MLSTM_HEADS + h
        state, m_state = ct_ref[h], m_ref[h][0:1, 0:1]
        for c in (reversed(range(nc)) if reverse else range(nc)):
            b = grow_ref[c][cf:cf + 1, :]
            cm = gcm_ref[c][cf:cf + 1, :]
            m_inter = b + m_state
            m_t = jnp.maximum(m_inter, b + cm)
            alpha = jnp.exp(b + cm - m_t)
            inter = jnp.exp(m_inter - m_t)
            carried = _dot(state.astype(BF16), qt[h, c])
            den = alpha * loc[h, c][dv:dv + 1, :] + inter * carried[dv:dv + 1, :]
            scale = 1.0 / jnp.maximum(jnp.abs(den), jnp.exp(-m_t))
            emit(h, c, (alpha * scale) * loc[h, c][0:dv, :] + (inter * scale) * carried[0:dv, :])
            b_last, g_max = b[:, last:last + 1], b[:, last:last + 1] + cm[:, last:last + 1]
            m_new = jnp.maximum(b_last + m_state, g_max)
            state = jnp.exp(b_last + m_state - m_new) * state + jnp.exp(g_max - m_new) * dct[h, c]
            m_state = m_new
        ct_ref[h] = state
        m_ref[h] = jnp.broadcast_to(m_state, m_ref.shape[1:])


def _scan_init(ct_ref, m_ref):
    @pl.when(pl.program_id(1) == 0)
    def _():
        ct_ref[...] = jnp.zeros_like(ct_ref)
        m_ref[...] = jnp.zeros_like(m_ref)


def _scan_bwd_kernel(qt_ref, k_ref, vt_ref, pexp_ref, kw_ref, grow_ref, gcm_ref, ht_ref, ct_ref, m_ref):
    _scan_init(ct_ref, m_ref)
    cl, dv = SCAN_CHUNK, MLSTM_V_DIM

    def emit(h, c, val):
        ht_ref[h * dv:(h + 1) * dv, c * cl:(c + 1) * cl] = val

    _scan_block(qt_ref, k_ref, vt_ref, pexp_ref, kw_ref, grow_ref, gcm_ref, ct_ref, m_ref, True, emit)


def _scan_fwd_kernel(qt_ref, k_ref, vt_ref, pexp_ref, kw_ref, grow_ref, gcm_ref, hb_ref, og_ref, hn_ref,
                     y_ref, ct_ref, m_ref):
    _scan_init(ct_ref, m_ref)
    cl, dv = SCAN_CHUNK, MLSTM_V_DIM

    def emit(h, c, val):
        rows, cols = slice(c * cl, (c + 1) * cl), slice(h * dv, (h + 1) * dv)
        hs = val + hb_ref[cols, rows]
        hs = hs * lax.rsqrt(jnp.mean(hs * hs, axis=0, keepdims=True) + RMS_EPS) * hn_ref[cols, :]
        y_ref[rows, cols] = (og_ref[rows, cols].astype(F32) * hs.T).astype(BF16)

    _scan_block(qt_ref, k_ref, vt_ref, pexp_ref, kw_ref, grow_ref, gcm_ref, ct_ref, m_ref, False, emit)


def _mlstm_scans(qt, k, vt, og, pexp, kw, grow, gcm, head_norm_lanes, batch, seq):
    rows = SCAN_ROWS
    per_row = seq // rows
    qw, vw, nh, cl = MLSTM_QK_WIDTH, MLSTM_V_WIDTH, MLSTM_HEADS, SCAN_CHUNK
    tokens = batch * seq
    cpb = rows // cl

    def specs(reverse):
        direction = 1 if reverse else 0
        blk = lambda b, i: b * per_row + (per_row - 1 - i if reverse else i)
        nat = lambda width: pl.BlockSpec((rows, width), lambda b, i: (blk(b, i), 0))
        trn = lambda height: pl.BlockSpec((height, rows), lambda b, i: (0, blk(b, i)))
        gate = pl.BlockSpec((cpb, MLSTM_GATES, cl), lambda b, i: (blk(b, i), 0, 0))
        common = [trn(qw), nat(qw), trn(nh * MLSTM_VT_ROWS),
                  pl.BlockSpec((None, cpb, nh, cl, cl), lambda b, i: (direction, blk(b, i), 0, 0, 0)),
                  pl.BlockSpec((None, rows, qw), lambda b, i: (direction, blk(b, i), 0)),
                  gate, gate]
        return common, nat, trn

    scratch = [pltpu.VMEM((nh, MLSTM_VT_ROWS, MLSTM_QK_DIM), F32),
               pltpu.VMEM((nh, 8, LANES), F32)]
    params = _params(("parallel", "arbitrary"))
    common, nat, trn = specs(True)
    ht_bw = pl.pallas_call(
        _scan_bwd_kernel,
        grid=(batch, per_row),
        in_specs=common,
        out_specs=trn(vw),
        out_shape=jax.ShapeDtypeStruct((vw, tokens), F32),
        scratch_shapes=scratch,
        compiler_params=params,
        name="mlstm_bwd",
    )(qt, k, vt, pexp, kw, grow, gcm)
    common, nat, trn = specs(False)
    return pl.pallas_call(
        _scan_fwd_kernel,
        grid=(batch, per_row),
        in_specs=common + [trn(vw), nat(vw), _const_spec((vw, LANES))],
        out_specs=nat(vw),
        out_shape=jax.ShapeDtypeStruct((tokens, vw), BF16),
        scratch_shapes=scratch,
        compiler_params=params,
        name="mlstm_fwd",
    )(qt, k, vt, pexp, kw, grow, gcm, ht_bw, og, head_norm_lanes)


def _l1_post_kernel(x_ref, y_ref, mod_ref, g2_ref, gf_ref, wout_ref, wgu_ref, wd_ref, out_ref, act_ref):
    mod = mod_ref[0]
    x1 = x_ref[...] + mod[2:3] * _dot(y_ref[...], wout_ref[...])
    x2 = _ffn(x1, mod, g2_ref[...], wgu_ref, wd_ref, act_ref)
    out_ref[...] = x2 * lax.rsqrt(jnp.mean(x2 * x2, axis=-1, keepdims=True) + RMS_EPS) * gf_ref[...]


def _l1_post(x2, y, mod, g2, g_final, w_out, w_gu, w_down, seq):
    tokens = x2.shape[0]
    tm = TOKEN_TILE
    per_row = seq // tm
    tile = lambda w: pl.BlockSpec((tm, w), lambda i: (i, 0))
    return pl.pallas_call(
        _l1_post_kernel,
        grid=(tokens // tm,),
        in_specs=[
            tile(D_MODEL), tile(MLSTM_V_WIDTH),
            pl.BlockSpec((1, 6, D_MODEL), lambda i: (i // per_row, 0, 0)),
            _const_spec((1, D_MODEL)),
            _const_spec((1, D_MODEL)),
            _const_spec((MLSTM_V_WIDTH, D_MODEL)),
            _const_spec((D_MODEL, 2 * FFN_HIDDEN)),
            _const_spec((FFN_HIDDEN, D_MODEL)),
        ],
        out_specs=tile(D_MODEL),
        out_shape=jax.ShapeDtypeStruct((tokens, D_MODEL), F32),
        scratch_shapes=[pltpu.VMEM((tm, FFN_HIDDEN), BF16)],
        compiler_params=_params(("parallel",)),
        name="l1_post",
    )(x2, y, mod, g2, g_final, w_out, w_gu, w_down)


def _trunk(x, mod0, mod1, weights, tables):
    batch, seq, _ = x.shape
    assert seq % TOKEN_TILE == 0 and seq % SCAN_ROWS == 0
    x2 = x.reshape(batch * seq, D_MODEL)
    qkv = _qkv_proj(x2, mod0, weights["l0_norm1"], weights["w_qkv"], tables, batch, seq)
    outs, lses = zip(*[_attention_group(qkv[g], g) for g in range(len(ATT_GROUPS))])
    x2 = _l0_post(x2, outs, lses, mod0, weights["l0_norm2"], weights["w_o"], weights["l0_w_gu"],
                  weights["l0_w_down"], seq)
    scan_inputs = _mlstm_in(x2, mod1, weights["l1_norm1"], weights, seq)
    y = _mlstm_scans(*scan_inputs, weights["head_norm_lanes"], batch, seq)
    out = _l1_post(x2, y, mod1, weights["l1_norm2"], weights["final_norm"], weights["w_out"],
                   weights["l1_w_gu"], weights["l1_w_down"], seq)
    return out.reshape(batch, seq, D_MODEL)


def kernel(x_prompt, x_sample, c_prompt, c_sample, l0_ada_w, l0_ada_b, l0_norm1, l0_attn_w_qkv, l0_attn_w_o, l0_norm2, l0_ffn_w_gu, l0_ffn_w_down, l1_ada_w, l1_ada_b, l1_norm1, l1_mlstm_w_in, l1_mlstm_b_gates, l1_mlstm_head_norm, l1_mlstm_w_out, l1_norm2, l1_ffn_w_gu, l1_ffn_w_down, final_norm):
    row = lambda g: g.reshape(1, -1).astype(F32)
    qw, vw = MLSTM_QK_WIDTH, MLSTM_V_WIDTH
    w_gate =jnp.pad(l1_mlstm_w_in[:, MLSTM_MAIN_WIDTH:], ((0, 0), (0, LANES - MLSTM_GATES)))
    wg1 = w_gate.astype(BF16)
    weights = {
        "l0_norm1": row(l0_norm1), "l0_norm2": row(l0_norm2), "l1_norm1": row(l1_norm1), "l1_norm2": row(l1_norm2),
        "final_norm": row(final_norm),
        "head_norm_lanes": jnp.broadcast_to(l1_mlstm_head_norm.astype(F32)[:, None], (MLSTM_V_WIDTH, LANES)),
        "w_qkv": l0_attn_w_qkv.astype(BF16), "w_o": l0_attn_w_o.astype(BF16),
        "l0_w_gu": l0_ffn_w_gu.astype(BF16), "l0_w_down": l0_ffn_w_down.astype(BF16),
        "wq_t": l1_mlstm_w_in[:, :qw].T.astype(BF16), "wk": l1_mlstm_w_in[:, qw:2 * qw].astype(BF16),
        "wv_t": l1_mlstm_w_in[:, 2 * qw:2 * qw + vw].T.astype(BF16),
        "wo_gate": l1_mlstm_w_in[:, 2 * qw + vw:MLSTM_MAIN_WIDTH].astype(BF16),
        "wg": jnp.concatenate([wg1, (w_gate - wg1.astype(F32)).astype(BF16)], axis=1),
        "b_gates": jnp.pad(l1_mlstm_b_gates, (0, LANES - MLSTM_GATES)).reshape(1, LANES).astype(F32),
        "w_out": l1_mlstm_w_out.astype(BF16),
        "l1_w_gu": l1_ffn_w_gu.astype(BF16), "l1_w_down": l1_ffn_w_down.astype(BF16),
    }
    nb = x_prompt.shape[0]
    c_all = jnp.concatenate([c_prompt, c_sample], axis=0)
    mod0 = _ada_mod(c_all, l0_ada_w, l0_ada_b)
    mod1 = _ada_mod(c_all, l1_ada_w, l1_ada_b)
    tables = _rope_tables(max(x_prompt.shape[1], x_sample.shape[1]))
    y_prompt = _trunk(x_prompt, mod0[:nb], mod1[:nb], weights, tables)
    y_sample = _trunk(x_sample, mod0[nb:], mod1[nb:], weights, tables)
    return (y_prompt, y_sample)
```

```python
import functools
import math

import jax
import jax.numpy as jnp
from jax import lax
from jax.experimental import pallas as pl
from jax.experimental.pallas import tpu as pltpu

F32 = jnp.float32
BF16 = jnp.bfloat16

D_MODEL = 1024
ATT_GROUPS = ((128, 1), (512, 4), (2048, 16))
ATT_HEADS = 8
ATT_HEAD_DIM = 64
ATT_GROUP_WIDTH = ATT_HEADS * ATT_HEAD_DIM
ATT_QKV_WIDTH = len(ATT_GROUPS) * 3 * ATT_GROUP_WIDTH
ATT_RADIUS = ATT_GROUPS[0][0] // (2 * ATT_GROUPS[0][1])
assert all(w // (2 * d) == ATT_RADIUS for w, d in ATT_GROUPS)
ROPE_DIM = ATT_HEAD_DIM // 4
ROPE_HALF = ROPE_DIM // 2
ROPE_THETA = 500000.0
MLSTM_HEADS = 4
MLSTM_QK_DIM = D_MODEL // (2 * MLSTM_HEADS)
MLSTM_V_DIM = D_MODEL // MLSTM_HEADS
MLSTM_QK_WIDTH = MLSTM_HEADS * MLSTM_QK_DIM
MLSTM_V_WIDTH = MLSTM_HEADS * MLSTM_V_DIM
MLSTM_MAIN_WIDTH = 2 * MLSTM_QK_WIDTH + 2 * MLSTM_V_WIDTH
MLSTM_GATES = 4 * MLSTM_HEADS
MLSTM_VT_ROWS = MLSTM_V_DIM + 16
FFN_HIDDEN = int(math.ceil(8 * D_MODEL / 3 / 256)) * 256
RMS_EPS = 1e-6

LANES = 128
TOKEN_TILE = 512
ATT_Q_BLOCK = 128
ATT_ROWS = 512
SCAN_CHUNK = 128
SCAN_ROWS = 512
FFN_CHUNK = 512
NEG_BIG = -1e30
VMEM_LIMIT = 56 * 1024 * 1024


def _params(semantics):
    return pltpu.CompilerParams(dimension_semantics=semantics, vmem_limit_bytes=VMEM_LIMIT)


def _const_spec(shape):
    zeros = (0,) * len(shape)
    return pl.BlockSpec(shape, lambda *_: zeros, pipeline_mode=pl.Buffered(1))


def _shifted(ntiles):
    return (lambda s: jnp.minimum(s, ntiles - 1)), (lambda s: jnp.maximum(s - 1, 0))


def _split3(x):
    a = x.astype(BF16)
    r = x - a.astype(F32)
    b = r.astype(BF16)
    c = (r - b.astype(F32)).astype(BF16)
    return a, b, c


def _dot(a, b):
    return jnp.dot(a, b, preferred_element_type=F32)


def _norm_mod(x, gain, scale, shift):
    y = x * lax.rsqrt(jnp.mean(x * x, axis=-1, keepdims=True) + RMS_EPS)
    return (y * gain) * (1.0 + scale) + shift


def _ada_kernel(c_ref, w_ref, b_ref, o_ref):
    c = c_ref[...]
    a = c * jax.nn.sigmoid(c)
    a1, a2, _ = _split3(a)
    w1, w2, _ = _split3(w_ref[...])
    o_ref[...] = _dot(a1, w1) + _dot(a2, w1) + _dot(a1, w2) + b_ref[...]


def _ada_mod(c, ada_w, ada_b):
    nb = c.shape[0]
    width = ada_w.shape[1]
    tile = width // 4
    out = pl.pallas_call(
        _ada_kernel,
        grid=(4,),
        in_specs=[
            pl.BlockSpec((nb, D_MODEL), lambda j: (0, 0)),
            pl.BlockSpec((D_MODEL, tile), lambda j: (0, j)),
            pl.BlockSpec((1, tile), lambda j: (0, j)),
        ],
        out_specs=pl.BlockSpec((nb, tile), lambda j: (0, j)),
        out_shape=jax.ShapeDtypeStruct((nb, width), F32),
        compiler_params=_params(("arbitrary",)),
        name="ada_mod",
    )(c, ada_w, ada_b.reshape(1, width))
    return out.reshape(nb, 6, D_MODEL)


def _rope_tables(seq):
    inv = jnp.float32(ROPE_THETA) ** (-jnp.arange(ROPE_HALF, dtype=F32) / ROPE_HALF)
    ang = jnp.arange(seq, dtype=F32)[:, None] * inv[None, :]
    cos, sin = jnp.cos(ang), jnp.sin(ang)
    rest = ATT_HEAD_DIM - ROPE_DIM
    zeros = jnp.zeros((seq, ROPE_HALF), F32)
    c = jnp.concatenate([cos, cos, jnp.ones((seq, rest), F32)], axis=1)
    s_up = jnp.concatenate([-sin, zeros, jnp.zeros((seq, rest), F32)], axis=1)
    s_dn = jnp.concatenate([zeros, sin, jnp.zeros((seq, rest), F32)], axis=1)
    reps = LANES // ATT_HEAD_DIM
    return tuple(jnp.tile(t, (1, reps)) for t in (c, s_up, s_dn))


def _phase_major(table, dil, tile):
    if dil == 1:
        return table
    seq, width = table.shape
    return table.reshape(seq // tile, tile // dil, dil, width).transpose(0, 2, 1, 3).reshape(seq, width)


def _qkv_kernel(x_ref, mod_ref, g_ref, w_ref, *rest):
    ngrp = len(ATT_GROUPS)
    tab_refs, o_refs, stage_ref = rest[:3 * ngrp], rest[3 * ngrp:4 * ngrp], rest[4 * ngrp]
    mod = mod_ref[0]
    h32 = _norm_mod(x_ref[...], g_ref[...], mod[1:2], mod[0:1])
    gw = ATT_GROUP_WIDTH
    tm = x_ref.shape[0]
    nslab = D_MODEL // LANES
    for s in range(nslab):
        stage_ref[s] = h32[:, s * LANES:(s + 1) * LANES]
    for grp in range(ngrp):
        dil = ATT_GROUPS[grp][1]
        o_ref = o_refs[grp]
        cos, s_up, s_dn = (t[...] for t in tab_refs[3 * grp:3 * grp + 3])
        if dil == 1:
            h = h32.astype(BF16)
        else:
            h = jnp.concatenate(
                [jnp.concatenate([stage_ref[s, pl.ds(ph, tm // dil, stride=dil), :] for s in range(nslab)], axis=1)
                 for ph in range(dil)], axis=0).astype(BF16)
        for kind in range(3):
            blk = 3 * grp + kind
            p = _dot(h, w_ref[:, blk * gw:(blk + 1) * gw])
            for s in range(gw // LANES):
                val = p[:, s * LANES:(s + 1) * LANES]
                if kind < 2:
                    val = val * cos + pltpu.roll(val, LANES - ROPE_HALF, 1) * s_up + pltpu.roll(val, ROPE_HALF, 1) * s_dn
                if kind == 0:
                    val = val * (ATT_HEAD_DIM ** -0.5)
                val = val.astype(BF16)
                lo = kind * gw + s * LANES
                rows = tm // dil
                for ph in range(dil):
                    o_ref[0, :, ph * 3 * gw + lo:ph * 3 * gw + lo + LANES] = val[ph * rows:(ph + 1) * rows]


def _qkv_proj(x2, mod, gain, w_qkv, tables, batch, seq):
    tokens = x2.shape[0]
    tm = TOKEN_TILE
    per_row = seq // tm
    tab_spec = pl.BlockSpec((tm, LANES), lambda i: (i % per_row, 0))
    gw3 = 3 * ATT_GROUP_WIDTH
    dils = [d for _, d in ATT_GROUPS]
    group_tables = [_phase_major(t[:seq], d, tm) for d in dils for t in tables]
    return pl.pallas_call(
        _qkv_kernel,
        grid=(tokens // tm,),
        in_specs=[
            pl.BlockSpec((tm, D_MODEL), lambda i: (i, 0)),
            pl.BlockSpec((1, 6, D_MODEL), lambda i: (i // per_row, 0, 0)),
            _const_spec((1, D_MODEL)),
            _const_spec((D_MODEL, ATT_QKV_WIDTH)),
        ] + [tab_spec] * len(group_tables),
        out_specs=[pl.BlockSpec((1, tm // d, d * gw3), lambda i: (i // per_row, i % per_row, 0)) for d in dils],
        out_shape=[jax.ShapeDtypeStruct((batch, seq // d, d * gw3), BF16) for d in dils],
        scratch_shapes=[pltpu.VMEM((D_MODEL // LANES, tm, LANES), F32)],
        compiler_params=_params(("parallel",)),
        name="qkv_proj",
    )(x2, mod, gain, w_qkv, *group_tables)


def _attn_kernel(q_ref, k_ref, kp_ref, kn_ref, v_ref, vp_ref, vn_ref, o_ref, lse_ref, kbuf, vbuf, *, length, rows):
    rad, bq = ATT_RADIUS, ATT_Q_BLOCK
    nk = bq + 2 * rad
    i = pl.program_id(2)
    kbuf[0:rad, :] = kp_ref[0]
    kbuf[rad:rad + rows, :] = k_ref[0]
    kbuf[rad + rows:, :] = kn_ref[0]
    vbuf[0:rad, :] = vp_ref[0]
    vbuf[rad:rad + rows, :] = v_ref[0]
    vbuf[rad + rows:, :] = vn_ref[0]

    lane = lax.broadcasted_iota(jnp.int32, (bq, LANES), 1)
    first = lane < ATT_HEAD_DIM
    head_mask = (jnp.where(first, 1.0, 0.0).astype(BF16), jnp.where(first, 0.0, 1.0).astype(BF16))
    t_idx = lax.broadcasted_iota(jnp.int32, (bq, nk), 0)
    c_idx = lax.broadcasted_iota(jnp.int32, (bq, nk), 1)
    rel = c_idx - rad - t_idx
    pairs = [(jb, hp) for jb in range(rows // bq) for hp in range(ATT_GROUP_WIDTH // LANES)]
    items = [(jb, hp, half) for jb, hp in pairs for half in range(2)]
    cols = lambda hp: slice(hp * LANES, (hp + 1) * LANES)
    bias = {}
    for jb in range(rows // bq):
        kpos = i * rows + jb * bq - rad + c_idx
        ok = (jnp.abs(rel) <= rad) & (kpos >= 0) & (kpos < length)
        bias[jb] = jnp.where(ok, 0.0, NEG_BIG)
    scores = {}
    for jb, hp, half in items:
        q2 = q_ref[0, jb * bq:(jb + 1) * bq, cols(hp)] * head_mask[half]
        scores[jb, hp, half] = _nt_dot(q2, kbuf[jb * bq:jb * bq + nk, cols(hp)]) + bias[jb]
    probs, den, lse = {}, {}, {}
    for it in items:
        m = jnp.max(scores[it], axis=1, keepdims=True)
        p = jnp.exp(scores[it] - m)
        den[it] = jnp.sum(p, axis=1, keepdims=True)
        lse[it] = m + jnp.log(den[it])
        probs[it] = p.astype(BF16)
    outs = {(jb, hp, half): _dot(probs[jb, hp, half], vbuf[jb * bq:jb * bq + nk, cols(hp)]) / den[jb, hp, half]
            for jb, hp, half in items}
    for jb, hp in pairs:
        o_ref[0, jb * bq:(jb + 1) * bq, cols(hp)] = jnp.where(first, outs[jb, hp, 0], outs[jb, hp, 1]).astype(BF16)
        lse_ref[0, jb * bq:(jb + 1) * bq, cols(hp)] = jnp.where(first, lse[jb, hp, 0], lse[jb, hp, 1])


def _attention_group(qkv, group):
    _, dil = ATT_GROUPS[group]
    batch, length, _ = qkv.shape
    rows, rad, gw = min(ATT_ROWS, length), ATT_RADIUS, ATT_GROUP_WIDTH
    assert length % rows == 0 and rows % ATT_Q_BLOCK == 0
    nblk = length // rows
    halo_per_blk = rows // rad
    last_halo = length // rad - 1
    view = qkv

    def main(kind):
        return pl.BlockSpec((1, rows, gw), lambda b, p, i: (b, i, 3 * p + kind))

    def prev(kind):
        return pl.BlockSpec((1, rad, gw), lambda b, p, i: (b, jnp.maximum(i * halo_per_blk - 1, 0), 3 * p + kind))

    def nxt(kind):
        return pl.BlockSpec((1, rad, gw), lambda b, p, i: (b, jnp.minimum((i + 1) * halo_per_blk, last_halo), 3 * p + kind))

    out_spec = pl.BlockSpec((1, rows, gw), lambda b, p, i: (b, i, p))
    return pl.pallas_call(
        functools.partial(_attn_kernel, length=length, rows=rows),
        grid=(batch, dil, nblk),
        in_specs=[main(0), main(1), prev(1), nxt(1), main(2), prev(2), nxt(2)],
        out_specs=[out_spec, out_spec],
        out_shape=[jax.ShapeDtypeStruct((batch, length, dil * gw), BF16),
                   jax.ShapeDtypeStruct((batch, length, dil * gw), F32)],
        scratch_shapes=[pltpu.VMEM((rows + 2 * rad, gw), BF16), pltpu.VMEM((rows + 2 * rad, gw), BF16)],
        compiler_params=_params(("parallel", "parallel", "parallel")),
        name=f"attn_g{group}",
    )(view, view, view, view, view, view, view)


ROW_PARTS = 2


def _ffn_input(x1, mod, g2):
    return _norm_mod(x1, g2, mod[4:5], mod[3:4]).astype(BF16)


def _ffn(x1, h, mod, wgu_ref, wd_ref, act_ref, rows):
    for lo in range(0, FFN_HIDDEN, FFN_CHUNK):
        hi = min(lo + FFN_CHUNK, FFN_HIDDEN)
        a = _dot(h, wgu_ref[:, lo:hi])
        b = _dot(h, wgu_ref[:, FFN_HIDDEN + lo:FFN_HIDDEN + hi])
        act_ref[rows, lo:hi] = (a * jax.nn.sigmoid(a) * b).astype(BF16)
    return x1 + mod[5:6] * _dot(act_ref[rows, :], wd_ref[...])


def _l0_post_kernel(x_ref, o0_ref, o1_ref, o2_ref, l0_ref, l1_ref, l2_ref, mod_ref, g2_ref,
                    wo_ref, wgu_ref, wd_ref, out_ref, act_ref, y_ref, stage_ref):
    mod = mod_ref[0]
    tm = x_ref.shape[0]
    gw = ATT_GROUP_WIDTH
    dils = [d for _, d in ATT_GROUPS]
    n = tm // ROW_PARTS
    prepared = []
    for part in range(ROW_PARTS):
        rows = slice(part * n, (part + 1) * n)
        slab = 0
        for s in range(gw // LANES):
            vals = []
            for ref, dil in zip((o0_ref, o1_ref, o2_ref, l0_ref, l1_ref, l2_ref), dils + dils):
                src_rows = slice(part * n // dil, (part + 1) * n // dil)
                if dil == 1:
                    vals.append(ref[0, src_rows, s * LANES:(s + 1) * LANES].astype(F32))
                    continue
                for ph in range(dil):
                    lo = ph * gw + s * LANES
                    stage_ref[slab, pl.ds(part * n + ph, n // dil, stride=dil), :] = (
                        ref[0, src_rows, lo:lo + LANES].astype(F32))
                vals.append(stage_ref[slab, rows, :])
                slab += 1
            o0, o1, o2, l0, l1, l2 = vals
            top = jnp.maximum(jnp.maximum(l0, l1), l2)
            e0, e1, e2 = jnp.exp(l0 - top), jnp.exp(l1 - top), jnp.exp(l2 - top)
            y_ref[rows, s * LANES:(s + 1) * LANES] = ((e0 * o0 + e1 * o1 + e2 * o2) / (e0 + e1 + e2)).astype(BF16)
        x1 = x_ref[rows, :] + mod[2:3] * _dot(y_ref[rows, :], wo_ref[...])
        prepared.append((rows, x1, _ffn_input(x1, mod, g2_ref[...])))
    for rows, x1, h in prepared:
        out_ref[rows, :] = _ffn(x1, h, mod, wgu_ref, wd_ref, act_ref, rows)


def _l0_post(x2, outs, lses, mod, g2, w_o, w_gu, w_down, seq):
    tokens = x2.shape[0]
    tm = TOKEN_TILE
    per_row = seq // tm
    gw = ATT_GROUP_WIDTH
    dils = [d for _, d in ATT_GROUPS]
    tile = lambda w: pl.BlockSpec((tm, w), lambda i: (i, 0))
    att_specs = [pl.BlockSpec((1, tm // d, d * gw), lambda i: (i // per_row, i % per_row, 0)) for d in dils]
    n_stage = 2 * (gw // LANES) * sum(1 for d in dils if d > 1)
    return pl.pallas_call(
        _l0_post_kernel,
        grid=(tokens // tm,),
        in_specs=[tile(D_MODEL)] + att_specs + att_specs + [
            pl.BlockSpec((1, 6, D_MODEL), lambda i: (i // per_row, 0, 0)),
            _const_spec((1, D_MODEL)),
            _const_spec((gw, D_MODEL)),
            _const_spec((D_MODEL, 2 * FFN_HIDDEN)),
            _const_spec((FFN_HIDDEN, D_MODEL)),
        ],
        out_specs=tile(D_MODEL),
        out_shape=jax.ShapeDtypeStruct((tokens, D_MODEL), F32),
        scratch_shapes=[pltpu.VMEM((tm, FFN_HIDDEN), BF16), pltpu.VMEM((tm, gw), BF16),
                        pltpu.VMEM((n_stage, tm, LANES), F32)],
        compiler_params=_params(("parallel",)),
        name="l0_post",
    )(x2, *outs, *lses, mod, g2, w_o, w_gu, w_down)


def _nt_dot(a, b):
    return lax.dot_general(a, b, (((1,), (1,)), ((), ())), preferred_element_type=F32)


def _mlstm_in_kernel(x_ref, mod_ref, g_ref, wqt_ref, wk_ref, wvt_ref, wo_ref, wg_ref, bg_ref,
                     qt_ref, k_ref, vt_ref, og_ref, pexp_ref, kw_ref, grow_ref, gcm_ref):
    mod = mod_ref[0]
    h32 = _norm_mod(x_ref[...], g_ref[...], mod[1:2], mod[0:1])
    h = h32.astype(BF16)
    dk, dv, vr = MLSTM_QK_DIM, MLSTM_V_DIM, MLSTM_VT_ROWS
    tm = x_ref.shape[0]
    ones_rows = jnp.where(lax.broadcasted_iota(jnp.int32, (vr - dv, tm), 0) == 0, 1.0, 0.0).astype(BF16)

    def project(part):
        if part == 0:
            qt_ref[...] = _nt_dot(wqt_ref[...], h).astype(BF16)
        elif part in (1, 2):
            for hd in range(2 * (part - 1), 2 * part):
                vt_ref[hd * vr:hd * vr + dv, :] = _nt_dot(wvt_ref[hd * dv:(hd + 1) * dv, :], h).astype(BF16)
                vt_ref[hd * vr + dv:(hd + 1) * vr, :] = ones_rows
        else:
            og_ref[...] = jax.nn.sigmoid(_dot(h, wo_ref[...])).astype(BF16)

    g2 = _dot(h, wg_ref[...])
    g = g2[:, 0:LANES] + g2[:, LANES:2 * LANES] + bg_ref[...]
    k32 = _dot(h, wk_ref[...]) * (dk ** -0.5)
    k_ref[...] = k32.astype(BF16)
    lane = lax.broadcasted_iota(jnp.int32, g.shape, 1)
    is_forget = (((lane >> 2) & 1) == 1) & (lane < MLSTM_GATES)
    log_sig = jnp.minimum(g, 0.0) - jnp.log(1.0 + jnp.exp(-jnp.abs(g)))
    g = jnp.where(is_forget, log_sig, g)
    cl, nh = SCAN_CHUNK, MLSTM_HEADS
    row = lax.broadcasted_iota(jnp.int32, (cl, cl), 0)
    col = lax.broadcasted_iota(jnp.int32, (cl, cl), 1)
    tri = jnp.where(col <= row, 1.0, 0.0).astype(BF16)
    lane_c = lax.broadcasted_iota(jnp.int32, (cl, LANES), 1)
    fwd_forget = (lane_c >= nh) & (lane_c < 2 * nh)
    bwd_forget = (lane_c >= 3 * nh) & (lane_c < 4 * nh)
    lane_r = lax.broadcasted_iota(jnp.int32, (MLSTM_GATES, cl), 1)
    row_r = lax.broadcasted_iota(jnp.int32, (MLSTM_GATES, cl), 0)
    for c in range(tm // cl):
        rows = slice(c * cl, (c + 1) * cl)
        gc = g[rows]
        g1, g2, g3 = _split3(gc)
        prefix = _dot(tri, g1) + _dot(tri, g2) + _dot(tri, g3)
        suffix = prefix[cl - 1:cl, :] - prefix + gc
        prepared = jnp.where(fwd_forget, prefix, jnp.where(bwd_forget, suffix, gc))
        w_cols = pltpu.roll(prepared, nh, 1) - prepared
        grow_ref[c] = prepared.T[0:MLSTM_GATES, :]
        w_rows = w_cols.T[0:MLSTM_GATES, :]
        run_fwd, run_bwd = w_rows, w_rows
        shift = 1
        while shift < cl:
            run_fwd = jnp.maximum(run_fwd, jnp.where(lane_r >= shift, pltpu.roll(run_fwd, shift, 1), NEG_BIG))
            run_bwd = jnp.maximum(run_bwd, jnp.where(lane_r < cl - shift, pltpu.roll(run_bwd, cl - shift, 1), NEG_BIG))
            shift *= 2
        cm = jnp.where(row_r < 2 * nh, run_fwd, run_bwd)
        gcm_ref[c] = cm
        for direction in range(2):
            keep = (row >= col) if direction else (row <= col)
            last = 0 if direction else cl - 1
            for hd in range(nh):
                cf = 2 * nh * direction + nh + hd
                w_col = w_cols[:, cf:cf + 1]
                pexp_ref[direction, c, hd] = jnp.exp(jnp.where(keep, w_col - cm[cf:cf + 1, :], NEG_BIG)).astype(BF16)
                w_exp = jnp.exp(w_col - cm[cf:cf + 1, last:last + 1])
                kw_ref[direction, rows, hd * dk:(hd + 1) * dk] = (k32[rows, hd * dk:(hd + 1) * dk] * w_exp).astype(BF16)
    for part in range(4):
        project(part)


def _mlstm_in(x2, mod, gain, weights, seq):
    tokens = x2.shape[0]
    tm = TOKEN_TILE
    per_row = seq // tm
    tile = lambda w: pl.BlockSpec((tm, w), lambda i: (i, 0))
    tile_t = lambda r: pl.BlockSpec((r, tm), lambda i: (0, i))
    qw, vw, nh, cl = MLSTM_QK_WIDTH, MLSTM_V_WIDTH, MLSTM_HEADS, SCAN_CHUNK
    cpt = tm // cl
    chunks = tokens // cl
    row_spec = pl.BlockSpec((cpt, MLSTM_GATES, cl), lambda i: (i, 0, 0))
    return pl.pallas_call(
        _mlstm_in_kernel,
        grid=(tokens // tm,),
        in_specs=[
            tile(D_MODEL),
            pl.BlockSpec((1, 6, D_MODEL), lambda i: (i // per_row, 0, 0)),
            _const_spec((1, D_MODEL)),
            _const_spec((qw, D_MODEL)),
            _const_spec((D_MODEL, qw)),
            _const_spec((vw, D_MODEL)),
            _const_spec((D_MODEL, vw)),
            _const_spec((D_MODEL, 2 * LANES)),
            _const_spec((1, LANES)),
        ],
        out_specs=[tile_t(qw), tile(qw), tile_t(nh * MLSTM_VT_ROWS), tile(vw),
                   pl.BlockSpec((2, cpt, nh, cl, cl), lambda i: (0, i, 0, 0, 0)),
                   pl.BlockSpec((2, tm, qw), lambda i: (0, i, 0)),
                   row_spec, row_spec],
        out_shape=[
            jax.ShapeDtypeStruct((qw, tokens), BF16),
            jax.ShapeDtypeStruct((tokens, qw), BF16),
            jax.ShapeDtypeStruct((nh * MLSTM_VT_ROWS, tokens), BF16),
            jax.ShapeDtypeStruct((tokens, vw), BF16),
            jax.ShapeDtypeStruct((2, chunks, nh, cl, cl), BF16),
            jax.ShapeDtypeStruct((2, tokens, qw), BF16),
            jax.ShapeDtypeStruct((chunks, MLSTM_GATES, cl), F32),
            jax.ShapeDtypeStruct((chunks, MLSTM_GATES, cl), F32),
        ],
        compiler_params=_params(("parallel",)),
        name="mlstm_in",
    )(x2, mod, gain, weights["wq_t"], weights["wk"], weights["wv_t"], weights["wo_gate"], weights["wg"],
      weights["b_gates"])


def _scan_block(qt_ref, k_ref, vt_ref, pexp_ref, kw_ref, grow_ref, gcm_ref, ct_ref, m_ref, reverse, emit):
    cl, dk, dv, vr = SCAN_CHUNK, MLSTM_QK_DIM, MLSTM_V_DIM, MLSTM_VT_ROWS
    nc = k_ref.shape[0] // cl
    direction = 1 if reverse else 0
    last = 0 if reverse else cl - 1
    items = [(h, c) for h in range(MLSTM_HEADS) for c in range(nc)]
    span = lambda c: slice(c * cl, (c + 1) * cl)
    qt = {(h, c): qt_ref[h * dk:(h + 1) * dk, span(c)] for h, c in items}
    vt = {(h, c): vt_ref[h * vr:(h + 1) * vr, span(c)] for h, c in items}

    p_t = {}
    for h, c in items:
        s_t = _dot(k_ref[span(c), h * dk:(h + 1) * dk], qt[h, c])
        p_t[h, c] = (s_t * pexp_ref[c, h].astype(F32)).astype(BF16)
    loc = {(h, c): _dot(vt[h, c], p_t[h, c]) for h, c in items}
    dct = {(h, c): _dot(vt[h, c], kw_ref[span(c), h * dk:(h + 1) * dk]) for h, c in items}

    heads = range(MLSTM_HEADS)
    state = {h: ct_ref[h] for h in heads}
    m_state = {h: m_ref[h][0:1, 0:1] for h in heads}
    for c in (reversed(range(nc)) if reverse else range(nc)):
        for h in heads:
            cf = 2 * MLSTM_HEADS * direction + MLSTM_HEADS + h
            b = grow_ref[c][cf:cf + 1, :]
            cm = gcm_ref[c][cf:cf + 1, :]
            m_inter = b + m_state[h]
            m_t = jnp.maximum(m_inter, b + cm)
            alpha = jnp.exp(b + cm - m_t)
            inter = jnp.exp(m_inter - m_t)
            carried = _dot(state[h].astype(BF16), qt[h, c])
            den = alpha * loc[h, c][dv:dv + 1, :] + inter * carried[dv:dv + 1, :]
            scale = 1.0 / jnp.maximum(jnp.abs(den), jnp.exp(-m_t))
            emit(h, c, (alpha * scale) * loc[h, c][0:dv, :] + (inter * scale) * carried[0:dv, :])
            b_last, g_max = b[:, last:last + 1], b[:, last:last + 1] + cm[:, last:last + 1]
            m_new = jnp.maximum(b_last + m_state[h], g_max)
            state[h] = jnp.exp(b_last + m_state[h] - m_new) * state[h] + jnp.exp(g_max - m_new) * dct[h, c]
            m_state[h] = m_new
    for h in heads:
        ct_ref[h] = state[h]
        m_ref[h] = jnp.broadcast_to(m_state[h], m_ref.shape[1:])


def _scan_init(ct_ref, m_ref):
    @pl.when(pl.program_id(1) == 0)
    def _():
        ct_ref[...] = jnp.zeros_like(ct_ref)
        m_ref[...] = jnp.zeros_like(m_ref)


def _scan_bwd_kernel(qt_ref, k_ref, vt_ref, pexp_ref, kw_ref, grow_ref, gcm_ref, ht_ref, ct_ref, m_ref):
    _scan_init(ct_ref, m_ref)
    cl, dv = SCAN_CHUNK, MLSTM_V_DIM

    def emit(h, c, val):
        ht_ref[h * dv:(h + 1) * dv, c * cl:(c + 1) * cl] = val

    _scan_block(qt_ref, k_ref, vt_ref, pexp_ref, kw_ref, grow_ref, gcm_ref, ct_ref, m_ref, True, emit)


def _scan_fwd_kernel(qt_ref, k_ref, vt_ref, pexp_ref, kw_ref, grow_ref, gcm_ref, hb_ref, og_ref, hn_ref,
                     y_ref, ct_ref, m_ref):
    _scan_init(ct_ref, m_ref)
    cl, dv = SCAN_CHUNK, MLSTM_V_DIM

    def emit(h, c, val):
        rows, cols = slice(c * cl, (c + 1) * cl), slice(h * dv, (h + 1) * dv)
        hs = val + hb_ref[cols, rows]
        hs = hs * lax.rsqrt(jnp.mean(hs * hs, axis=0, keepdims=True) + RMS_EPS) * hn_ref[cols, :]
        y_ref[rows, cols] = (og_ref[rows, cols].astype(F32) * hs.T).astype(BF16)

    _scan_block(qt_ref, k_ref, vt_ref, pexp_ref, kw_ref, grow_ref, gcm_ref, ct_ref, m_ref, False, emit)


def _mlstm_scans(qt, k, vt, og, pexp, kw, grow, gcm, head_norm_lanes, batch, seq):
    rows = SCAN_ROWS
    per_row = seq // rows
    qw, vw, nh, cl = MLSTM_QK_WIDTH, MLSTM_V_WIDTH, MLSTM_HEADS, SCAN_CHUNK
    tokens = batch * seq
    cpb = rows // cl

    def specs(reverse):
        direction = 1 if reverse else 0
        blk = lambda b, i: b * per_row + (per_row - 1 - i if reverse else i)
        nat = lambda width: pl.BlockSpec((rows, width), lambda b, i: (blk(b, i), 0))
        trn = lambda height: pl.BlockSpec((height, rows), lambda b, i: (0, blk(b, i)))
        gate = pl.BlockSpec((cpb, MLSTM_GATES, cl), lambda b, i: (blk(b, i), 0, 0))
        common = [trn(qw), nat(qw), trn(nh * MLSTM_VT_ROWS),
                  pl.BlockSpec((None, cpb, nh, cl, cl), lambda b, i: (direction, blk(b, i), 0, 0, 0)),
                  pl.BlockSpec((None, rows, qw), lambda b, i: (direction, blk(b, i), 0)),
                  gate, gate]
        return common, nat, trn

    scratch = [pltpu.VMEM((nh, MLSTM_VT_ROWS, MLSTM_QK_DIM), F32),
               pltpu.VMEM((nh, 8, LANES), F32)]
    params = _params(("parallel", "arbitrary"))
    common, nat, trn = specs(True)
    ht_bw = pl.pallas_call(
        _scan_bwd_kernel,
        grid=(batch, per_row),
        in_specs=common,
        out_specs=trn(vw),
        out_shape=jax.ShapeDtypeStruct((vw, tokens), F32),
        scratch_shapes=scratch,
        compiler_params=params,
        name="mlstm_bwd",
    )(qt, k, vt, pexp, kw, grow, gcm)
    common, nat, trn = specs(False)
    return pl.pallas_call(
        _scan_fwd_kernel,
        grid=(batch, per_row),
        in_specs=common + [trn(vw), nat(vw), _const_spec((vw, LANES))],
        out_specs=nat(vw),
        out_shape=jax.ShapeDtypeStruct((tokens, vw), BF16),
        scratch_shapes=scratch,
        compiler_params=params,
        name="mlstm_fwd",
    )(qt, k, vt, pexp, kw, grow, gcm, ht_bw, og, head_norm_lanes)


def _l1_post_kernel(x_ref, y_ref, mod_ref, g2_ref, gf_ref, wout_ref, wgu_ref, wd_ref, out_ref, act_ref):
    mod = mod_ref[0]
    n = x_ref.shape[0] // ROW_PARTS
    prepared = []
    for part in range(ROW_PARTS):
        rows = slice(part * n, (part + 1) * n)
        x1 = x_ref[rows, :] + mod[2:3] * _dot(y_ref[rows, :], wout_ref[...])
        prepared.append((rows, x1, _ffn_input(x1, mod, g2_ref[...])))
    for rows, x1, h in prepared:
        x2 = _ffn(x1, h, mod, wgu_ref, wd_ref, act_ref, rows)
        out_ref[rows, :] = x2 * lax.rsqrt(jnp.mean(x2 * x2, axis=-1, keepdims=True) + RMS_EPS) * gf_ref[...]


def _l1_post(x2, y, mod, g2, g_final, w_out, w_gu, w_down, seq):
    tokens = x2.shape[0]
    tm = TOKEN_TILE
    per_row = seq // tm
    tile = lambda w: pl.BlockSpec((tm, w), lambda i: (i, 0))
    return pl.pallas_call(
        _l1_post_kernel,
        grid=(tokens // tm,),
        in_specs=[
            tile(D_MODEL), tile(MLSTM_V_WIDTH),
            pl.BlockSpec((1, 6, D_MODEL), lambda i: (i // per_row, 0, 0)),
            _const_spec((1, D_MODEL)),
            _const_spec((1, D_MODEL)),
            _const_spec((MLSTM_V_WIDTH, D_MODEL)),
            _const_spec((D_MODEL, 2 * FFN_HIDDEN)),
            _const_spec((FFN_HIDDEN, D_MODEL)),
        ],
        out_specs=tile(D_MODEL),
        out_shape=jax.ShapeDtypeStruct((tokens, D_MODEL), F32),
        scratch_shapes=[pltpu.VMEM((tm, FFN_HIDDEN), BF16)],
        compiler_params=_params(("parallel",)),
        name="l1_post",
    )(x2, y, mod, g2, g_final, w_out, w_gu, w_down)


def _trunk(x, mod0, mod1, weights, tables):
    batch, seq, _ = x.shape
    assert seq % TOKEN_TILE == 0 and seq % SCAN_ROWS == 0
    x2 = x.reshape(batch * seq, D_MODEL)
    qkv = _qkv_proj(x2, mod0, weights["l0_norm1"], weights["w_qkv"], tables, batch, seq)
    outs, lses = zip(*[_attention_group(qkv[g], g) for g in range(len(ATT_GROUPS))])
    x2 = _l0_post(x2, outs, lses, mod0, weights["l0_norm2"], weights["w_o"], weights["l0_w_gu"],
                  weights["l0_w_down"], seq)
    scan_inputs = _mlstm_in(x2, mod1, weights["l1_norm1"], weights, seq)
    y = _mlstm_scans(*scan_inputs, weights["head_norm_lanes"], batch, seq)
    out = _l1_post(x2, y, mod1, weights["l1_norm2"], weights["final_norm"], weights["w_out"],
                   weights["l1_w_gu"], weights["l1_w_down"], seq)
    return out.reshape(batch, seq, D_MODEL)


def kernel(x_prompt, x_sample, c_prompt, c_sample, l0_ada_w, l0_ada_b, l0_norm1, l0_attn_w_qkv, l0_attn_w_o, l0_norm2, l0_ffn_w_gu, l0_ffn_w_down, l1_ada_w, l1_ada_b, l1_norm1, l1_mlstm_w_in, l1_mlstm_b_gates, l1_mlstm_head_norm, l1_mlstm_w_out, l1_norm2, l1_ffn_w_gu, l1_ffn_w_down, final_norm):
    row = lambda g: g.reshape(1, -1).astype(F32)
    qw, vw = MLSTM_QK_WIDTH, MLSTM_V_WIDTH
    w_gate =jnp.pad(l1_mlstm_w_in[:, MLSTM_MAIN_WIDTH:], ((0, 0), (0, LANES - MLSTM_GATES)))
    wg1 = w_gate.astype(BF16)
    weights = {
        "l0_norm1": row(l0_norm1), "l0_norm2": row(l0_norm2), "l1_norm1": row(l1_norm1), "l1_norm2": row(l1_norm2),
        "final_norm": row(final_norm),
        "head_norm_lanes": jnp.broadcast_to(l1_mlstm_head_norm.astype(F32)[:, None], (MLSTM_V_WIDTH, LANES)),
        "w_qkv": l0_attn_w_qkv.astype(BF16), "w_o": l0_attn_w_o.astype(BF16),
        "l0_w_gu": l0_ffn_w_gu.astype(BF16), "l0_w_down": l0_ffn_w_down.astype(BF16),
        "wq_t": l1_mlstm_w_in[:, :qw].T.astype(BF16), "wk": l1_mlstm_w_in[:, qw:2 * qw].astype(BF16),
        "wv_t": l1_mlstm_w_in[:, 2 * qw:2 * qw + vw].T.astype(BF16),
        "wo_gate": l1_mlstm_w_in[:, 2 * qw + vw:MLSTM_MAIN_WIDTH].astype(BF16),
        "wg": jnp.concatenate([wg1, (w_gate - wg1.astype(F32)).astype(BF16)], axis=1),
        "b_gates": jnp.pad(l1_mlstm_b_gates, (0, LANES - MLSTM_GATES)).reshape(1, LANES).astype(F32),
        "w_out": l1_mlstm_w_out.astype(BF16),
        "l1_w_gu": l1_ffn_w_gu.astype(BF16), "l1_w_down": l1_ffn_w_down.astype(BF16),
    }
    nb = x_prompt.shape[0]
    c_all = jnp.concatenate([c_prompt, c_sample], axis=0)
    mod0 = _ada_mod(c_all, l0_ada_w, l0_ada_b)
    mod1 = _ada_mod(c_all, l1_ada_w, l1_ada_b)
    tables = _rope_tables(max(x_prompt.shape[1], x_sample.shape[1]))
    y_prompt = _trunk(x_prompt, mod0[:nb], mod1[:nb], weights, tables)
    y_sample = _trunk(x_sample, mod0[nb:], mod1[nb:], weights, tables)
    return (y_prompt, y_sample)
```

```python
import functools
import math

import jax
import jax.numpy as jnp
from jax import lax
from jax.experimental import pallas as pl
from jax.experimental.pallas import tpu as pltpu

F32 = jnp.float32
BF16 = jnp.bfloat16

D_MODEL = 1024
ATT_GROUPS = ((128, 1), (512, 4), (2048, 16))
ATT_HEADS = 8
ATT_HEAD_DIM = 64
ATT_GROUP_WIDTH = ATT_HEADS * ATT_HEAD_DIM
ATT_QKV_WIDTH = len(ATT_GROUPS) * 3 * ATT_GROUP_WIDTH
ATT_RADIUS = ATT_GROUPS[0][0] // (2 * ATT_GROUPS[0][1])
assert all(w // (2 * d) == ATT_RADIUS for w, d in ATT_GROUPS)
ROPE_DIM = ATT_HEAD_DIM // 4
ROPE_HALF = ROPE_DIM // 2
ROPE_THETA = 500000.0
MLSTM_HEADS = 4
MLSTM_QK_DIM = D_MODEL // (2 * MLSTM_HEADS)
MLSTM_V_DIM = D_MODEL // MLSTM_HEADS
MLSTM_QK_WIDTH = MLSTM_HEADS * MLSTM_QK_DIM
MLSTM_V_WIDTH = MLSTM_HEADS * MLSTM_V_DIM
MLSTM_MAIN_WIDTH = 2 * MLSTM_QK_WIDTH + 2 * MLSTM_V_WIDTH
MLSTM_GATES = 4 * MLSTM_HEADS
MLSTM_VT_ROWS = MLSTM_V_DIM + 16
FFN_HIDDEN = int(math.ceil(8 * D_MODEL / 3 / 256)) * 256
RMS_EPS = 1e-6

LANES = 128
TOKEN_TILE = 512
MLSTM_IN_TILE = 1024
ATT_Q_BLOCK = 128
ATT_ROWS = 1024
ATT_STAGE_BLOCKS = 1
SCAN_CHUNK = 128
SCAN_ROWS = 1024
FFN_CHUNK = 512
NEG_BIG = -1e30
VMEM_LIMIT = 56 * 1024 * 1024


def _params(semantics):
    return pltpu.CompilerParams(dimension_semantics=semantics, vmem_limit_bytes=VMEM_LIMIT)


def _const_spec(shape):
    zeros = (0,) * len(shape)
    return pl.BlockSpec(shape, lambda *_: zeros, pipeline_mode=pl.Buffered(1))


def _split3(x):
    a = x.astype(BF16)
    r = x - a.astype(F32)
    b = r.astype(BF16)
    c = (r - b.astype(F32)).astype(BF16)
    return a, b, c


def _dot(a, b):
    return jnp.dot(a, b, preferred_element_type=F32)


def _norm_mod(x, gain, scale, shift):
    y = x * lax.rsqrt(jnp.mean(x * x, axis=-1, keepdims=True) + RMS_EPS)
    return (y * gain) * (1.0 + scale) + shift


def _ada_kernel(c_ref, w_ref, b_ref, o_ref):
    c = c_ref[...]
    a = c * jax.nn.sigmoid(c)
    a1, a2, _ = _split3(a)
    w1, w2, _ = _split3(w_ref[...])
    o_ref[...] = _dot(a1, w1) + _dot(a2, w1) + _dot(a1, w2) + b_ref[...]


def _ada_mod(c, ada_w, ada_b):
    nb = c.shape[0]
    width = ada_w.shape[1]
    tile = width // 4
    out = pl.pallas_call(
        _ada_kernel,
        grid=(4,),
        in_specs=[
            pl.BlockSpec((nb, D_MODEL), lambda j: (0, 0)),
            pl.BlockSpec((D_MODEL, tile), lambda j: (0, j)),
            pl.BlockSpec((1, tile), lambda j: (0, j)),
        ],
        out_specs=pl.BlockSpec((nb, tile), lambda j: (0, j)),
        out_shape=jax.ShapeDtypeStruct((nb, width), F32),
        compiler_params=_params(("arbitrary",)),
        name="ada_mod",
    )(c, ada_w, ada_b.reshape(1, width))
    return out.reshape(nb, 6, D_MODEL)


def _rope_tables(seq):
    inv = jnp.float32(ROPE_THETA) ** (-jnp.arange(ROPE_HALF, dtype=F32) / ROPE_HALF)
    ang = jnp.arange(seq, dtype=F32)[:, None] * inv[None, :]
    cos, sin = jnp.cos(ang), jnp.sin(ang)
    rest = ATT_HEAD_DIM - ROPE_DIM
    zeros = jnp.zeros((seq, ROPE_HALF), F32)
    c = jnp.concatenate([cos, cos, jnp.ones((seq, rest), F32)], axis=1)
    s_up = jnp.concatenate([-sin, zeros, jnp.zeros((seq, rest), F32)], axis=1)
    s_dn = jnp.concatenate([zeros, sin, jnp.zeros((seq, rest), F32)], axis=1)
    reps = LANES // ATT_HEAD_DIM
    return tuple(jnp.tile(t, (1, reps)) for t in (c, s_up, s_dn))


def _phase_major(table, dil, tile):
    if dil == 1:
        return table
    seq, width = table.shape
    return table.reshape(seq // tile, tile // dil, dil, width).transpose(0, 2, 1, 3).reshape(seq, width)


def _qkv_kernel(x_ref, mod_ref, g_ref, w_ref, *rest):
    ngrp = len(ATT_GROUPS)
    tab_refs, o_refs, stage_ref = rest[:3 * ngrp], rest[3 * ngrp:4 * ngrp], rest[4 * ngrp]
    mod = mod_ref[0]
    h32 = _norm_mod(x_ref[...], g_ref[...], mod[1:2], mod[0:1])
    gw = ATT_GROUP_WIDTH
    tm = x_ref.shape[0]
    nslab = D_MODEL // LANES
    for s in range(nslab):
        stage_ref[s] = h32[:, s * LANES:(s + 1) * LANES]
    for grp in range(ngrp):
        dil = ATT_GROUPS[grp][1]
        o_ref = o_refs[grp]
        cos, s_up, s_dn = (t[...] for t in tab_refs[3 * grp:3 * grp + 3])
        if dil == 1:
            h = h32.astype(BF16)
        else:
            h = jnp.concatenate(
                [jnp.concatenate([stage_ref[s, pl.ds(ph, tm // dil, stride=dil), :] for s in range(nslab)], axis=1)
                 for ph in range(dil)], axis=0).astype(BF16)
        for kind in range(3):
            blk = 3 * grp + kind
            p = _dot(h, w_ref[:, blk * gw:(blk + 1) * gw])
            for s in range(gw // LANES):
                val = p[:, s * LANES:(s + 1) * LANES]
                if kind < 2:
                    val = val * cos + pltpu.roll(val, LANES - ROPE_HALF, 1) * s_up + pltpu.roll(val, ROPE_HALF, 1) * s_dn
                if kind == 0:
                    val = val * (ATT_HEAD_DIM ** -0.5)
                val = val.astype(BF16)
                lo = kind * gw + s * LANES
                rows = tm // dil
                for ph in range(dil):
                    o_ref[0, :, ph * 3 * gw + lo:ph * 3 * gw + lo + LANES] = val[ph * rows:(ph + 1) * rows]


def _qkv_proj(x2, mod, gain, w_qkv, tables, batch, seq):
    tokens = x2.shape[0]
    tm = TOKEN_TILE
    per_row = seq // tm
    tab_spec = pl.BlockSpec((tm, LANES), lambda i: (i % per_row, 0))
    gw3 = 3 * ATT_GROUP_WIDTH
    dils = [d for _, d in ATT_GROUPS]
    group_tables = [_phase_major(t[:seq], d, tm) for d in dils for t in tables]
    return pl.pallas_call(
        _qkv_kernel,
        grid=(tokens // tm,),
        in_specs=[
            pl.BlockSpec((tm, D_MODEL), lambda i: (i, 0)),
            pl.BlockSpec((1, 6, D_MODEL), lambda i: (i // per_row, 0, 0)),
            _const_spec((1, D_MODEL)),
            _const_spec((D_MODEL, ATT_QKV_WIDTH)),
        ] + [tab_spec] * len(group_tables),
        out_specs=[pl.BlockSpec((1, tm // d, d * gw3), lambda i: (i // per_row, i % per_row, 0)) for d in dils],
        out_shape=[jax.ShapeDtypeStruct((batch, seq // d, d * gw3), BF16) for d in dils],
        scratch_shapes=[pltpu.VMEM((D_MODEL // LANES, tm, LANES), F32)],
        compiler_params=_params(("parallel",)),
        name="qkv_proj",
    )(x2, mod, gain, w_qkv, *group_tables)


def _attn_kernel(q_ref, k_ref, kp_ref, kn_ref, v_ref, vp_ref, vn_ref, o_ref, lse_ref, kbuf, vbuf, *, length, rows):
    rad, bq = ATT_RADIUS, ATT_Q_BLOCK
    nk = bq + 2 * rad
    i = pl.program_id(2)
    kbuf[0:rad, :] = kp_ref[0]
    kbuf[rad:rad + rows, :] = k_ref[0]
    kbuf[rad + rows:, :] = kn_ref[0]
    vbuf[0:rad, :] = vp_ref[0]
    vbuf[rad:rad + rows, :] = v_ref[0]
    vbuf[rad + rows:, :] = vn_ref[0]

    lane = lax.broadcasted_iota(jnp.int32, (bq, LANES), 1)
    first = lane < ATT_HEAD_DIM
    head_mask = (jnp.where(first, 1.0, 0.0).astype(BF16), jnp.where(first, 0.0, 1.0).astype(BF16))
    t_idx = lax.broadcasted_iota(jnp.int32, (bq, nk), 0)
    c_idx = lax.broadcasted_iota(jnp.int32, (bq, nk), 1)
    rel = c_idx - rad - t_idx
    cols = lambda hp: slice(hp * LANES, (hp + 1) * LANES)
    blocks = list(range(rows // bq))
    for batch in [blocks[j:j + ATT_STAGE_BLOCKS] for j in range(0, len(blocks), ATT_STAGE_BLOCKS)]:
        pairs = [(jb, hp) for jb in batch for hp in range(ATT_GROUP_WIDTH // LANES)]
        items = [(jb, hp, half) for jb, hp in pairs for half in range(2)]
        bias = {}
        for jb in batch:
            kpos = i * rows + jb * bq - rad + c_idx
            ok = (jnp.abs(rel) <= rad) & (kpos >= 0) & (kpos < length)
            bias[jb] = jnp.where(ok, 0.0, NEG_BIG)
        scores = {}
        for jb, hp, half in items:
            q2 = q_ref[0, jb * bq:(jb + 1) * bq, cols(hp)] * head_mask[half]
            scores[jb, hp, half] = _nt_dot(q2, kbuf[jb * bq:jb * bq + nk, cols(hp)]) + bias[jb]
        probs, den, lse = {}, {}, {}
        for it in items:
            m = jnp.max(scores[it], axis=1, keepdims=True)
            p = jnp.exp(scores[it] - m)
            den[it] = jnp.sum(p, axis=1, keepdims=True)
            lse[it] = m + jnp.log(den[it])
            probs[it] = p.astype(BF16)
        outs = {(jb, hp, half): _dot(probs[jb, hp, half], vbuf[jb * bq:jb * bq + nk, cols(hp)]) / den[jb, hp, half]
                for jb, hp, half in items}
        for jb, hp in pairs:
            o_ref[0, jb * bq:(jb + 1) * bq, cols(hp)] = jnp.where(first, outs[jb, hp, 0], outs[jb, hp, 1]).astype(BF16)
            lse_ref[0, jb * bq:(jb + 1) * bq, cols(hp)] = jnp.where(first, lse[jb, hp, 0], lse[jb, hp, 1])


def _attention_group(qkv, group):
    _, dil = ATT_GROUPS[group]
    batch, length, _ = qkv.shape
    rows, rad, gw = min(ATT_ROWS, length), ATT_RADIUS, ATT_GROUP_WIDTH
    assert length % rows == 0 and rows % ATT_Q_BLOCK == 0
    nblk = length // rows
    halo_per_blk = rows // rad
    last_halo = length // rad - 1
    view = qkv

    def main(kind):
        return pl.BlockSpec((1, rows, gw), lambda b, p, i: (b, i, 3 * p + kind))

    def prev(kind):
        return pl.BlockSpec((1, rad, gw), lambda b, p, i: (b, jnp.maximum(i * halo_per_blk - 1, 0), 3 * p + kind))

    def nxt(kind):
        return pl.BlockSpec((1, rad, gw), lambda b, p, i: (b, jnp.minimum((i + 1) * halo_per_blk, last_halo), 3 * p + kind))

    out_spec = pl.BlockSpec((1, rows, gw), lambda b, p, i: (b, i, p))
    return pl.pallas_call(
        functools.partial(_attn_kernel, length=length, rows=rows),
        grid=(batch, dil, nblk),
        in_specs=[main(0), main(1), prev(1), nxt(1), main(2), prev(2), nxt(2)],
        out_specs=[out_spec, out_spec],
        out_shape=[jax.ShapeDtypeStruct((batch, length, dil * gw), BF16),
                   jax.ShapeDtypeStruct((batch, length, dil * gw), F32)],
        scratch_shapes=[pltpu.VMEM((rows + 2 * rad, gw), BF16), pltpu.VMEM((rows + 2 * rad, gw), BF16)],
        compiler_params=_params(("parallel", "parallel", "parallel")),
        name=f"attn_g{group}",
    )(view, view, view, view, view, view, view)


ROW_PARTS = 2


def _ffn_input(x1, mod, g2):
    return _norm_mod(x1, g2, mod[4:5], mod[3:4]).astype(BF16)


def _ffn(x1, h, mod, wgu_ref, wd_ref, act_ref, rows):
    for lo in range(0, FFN_HIDDEN, FFN_CHUNK):
        hi = min(lo + FFN_CHUNK, FFN_HIDDEN)
        a = _dot(h, wgu_ref[:, lo:hi])
        b = _dot(h, wgu_ref[:, FFN_HIDDEN + lo:FFN_HIDDEN + hi])
        act_ref[rows, lo:hi] = (a * jax.nn.sigmoid(a) * b).astype(BF16)
    return x1 + mod[5:6] * _dot(act_ref[rows, :], wd_ref[...])


def _l0_post_kernel(x_ref, o0_ref, o1_ref, o2_ref, l0_ref, l1_ref, l2_ref, mod_ref, g2_ref,
                    wo_ref, wgu_ref, wd_ref, out_ref, act_ref, y_ref, stage_ref):
    mod = mod_ref[0]
    tm = x_ref.shape[0]
    gw = ATT_GROUP_WIDTH
    dils = [d for _, d in ATT_GROUPS]
    n = tm // ROW_PARTS
    prepared = []
    for part in range(ROW_PARTS):
        rows = slice(part * n, (part + 1) * n)
        slab = 0
        for s in range(gw // LANES):
            vals = []
            for ref, dil in zip((o0_ref, o1_ref, o2_ref, l0_ref, l1_ref, l2_ref), dils + dils):
                src_rows = slice(part * n // dil, (part + 1) * n // dil)
                if dil == 1:
                    vals.append(ref[0, src_rows, s * LANES:(s + 1) * LANES].astype(F32))
                    continue
                for ph in range(dil):
                    lo = ph * gw + s * LANES
                    stage_ref[slab, pl.ds(part * n + ph, n // dil, stride=dil), :] = (
                        ref[0, src_rows, lo:lo + LANES].astype(F32))
                vals.append(stage_ref[slab, rows, :])
                slab += 1
            o0, o1, o2, l0, l1, l2 = vals
            top = jnp.maximum(jnp.maximum(l0, l1), l2)
            e0, e1, e2 = jnp.exp(l0 - top), jnp.exp(l1 - top), jnp.exp(l2 - top)
            y_ref[rows, s * LANES:(s + 1) * LANES] = ((e0 * o0 + e1 * o1 + e2 * o2) / (e0 + e1 + e2)).astype(BF16)
        x1 = x_ref[rows, :] + mod[2:3] * _dot(y_ref[rows, :], wo_ref[...])
        prepared.append((rows, x1, _ffn_input(x1, mod, g2_ref[...])))
    for rows, x1, h in prepared:
        out_ref[rows, :] = _ffn(x1, h, mod, wgu_ref, wd_ref, act_ref, rows)


def _l0_post(x2, outs, lses, mod, g2, w_o, w_gu, w_down, seq):
    tokens = x2.shape[0]
    tm = TOKEN_TILE
    per_row = seq // tm
    gw = ATT_GROUP_WIDTH
    dils = [d for _, d in ATT_GROUPS]
    tile = lambda w: pl.BlockSpec((tm, w), lambda i: (i, 0))
    att_specs = [pl.BlockSpec((1, tm // d, d * gw), lambda i: (i // per_row, i % per_row, 0)) for d in dils]
    n_stage = 2 * (gw // LANES) * sum(1 for d in dils if d > 1)
    return pl.pallas_call(
        _l0_post_kernel,
        grid=(tokens // tm,),
        in_specs=[tile(D_MODEL)] + att_specs + att_specs + [
            pl.BlockSpec((1, 6, D_MODEL), lambda i: (i // per_row, 0, 0)),
            _const_spec((1, D_MODEL)),
            _const_spec((gw, D_MODEL)),
            _const_spec((D_MODEL, 2 * FFN_HIDDEN)),
            _const_spec((FFN_HIDDEN, D_MODEL)),
        ],
        out_specs=tile(D_MODEL),
        out_shape=jax.ShapeDtypeStruct((tokens, D_MODEL), F32),
        scratch_shapes=[pltpu.VMEM((tm, FFN_HIDDEN), BF16), pltpu.VMEM((tm, gw), BF16),
                        pltpu.VMEM((n_stage, tm, LANES), F32)],
        compiler_params=_params(("parallel",)),
        name="l0_post",
    )(x2, *outs, *lses, mod, g2, w_o, w_gu, w_down)


def _nt_dot(a, b):
    return lax.dot_general(a, b, (((1,), (1,)), ((), ())), preferred_element_type=F32)


def _mlstm_in_kernel(x_ref, mod_ref, g_ref, wqt_ref, wk_ref, wvt_ref, wo_ref, wg_ref, bg_ref,
                     qt_ref, k_ref, vt_ref, og_ref, pexp_ref, kw_ref, grow_ref, gcm_ref):
    mod = mod_ref[0]
    h32 = _norm_mod(x_ref[...], g_ref[...], mod[1:2], mod[0:1])
    h = h32.astype(BF16)
    dk, dv, vr = MLSTM_QK_DIM, MLSTM_V_DIM, MLSTM_VT_ROWS
    tm = x_ref.shape[0]
    ones_rows = jnp.where(lax.broadcasted_iota(jnp.int32, (vr - dv, tm), 0) == 0, 1.0, 0.0).astype(BF16)

    def project(part):
        if part == 0:
            qt_ref[...] = _nt_dot(wqt_ref[...], h).astype(BF16)
        elif part in (1, 2):
            for hd in range(2 * (part - 1), 2 * part):
                vt_ref[hd * vr:hd * vr + dv, :] = _nt_dot(wvt_ref[hd * dv:(hd + 1) * dv, :], h).astype(BF16)
                vt_ref[hd * vr + dv:(hd + 1) * vr, :] = ones_rows
        else:
            og_ref[...] = jax.nn.sigmoid(_dot(h, wo_ref[...])).astype(BF16)

    g2 = _dot(h, wg_ref[...])
    g = g2[:, 0:LANES] + g2[:, LANES:2 * LANES] + bg_ref[...]
    k32 = _dot(h, wk_ref[...]) * (dk ** -0.5)
    k_ref[...] = k32.astype(BF16)
    lane = lax.broadcasted_iota(jnp.int32, g.shape, 1)
    is_forget = (((lane >> 2) & 1) == 1) & (lane < MLSTM_GATES)
    log_sig = jnp.minimum(g, 0.0) - jnp.log(1.0 + jnp.exp(-jnp.abs(g)))
    g = jnp.where(is_forget, log_sig, g)
    cl, nh = SCAN_CHUNK, MLSTM_HEADS
    row = lax.broadcasted_iota(jnp.int32, (cl, cl), 0)
    col = lax.broadcasted_iota(jnp.int32, (cl, cl), 1)
    tri = jnp.where(col <= row, 1.0, 0.0).astype(BF16)
    lane_c = lax.broadcasted_iota(jnp.int32, (cl, LANES), 1)
    fwd_forget = (lane_c >= nh) & (lane_c < 2 * nh)
    bwd_forget = (lane_c >= 3 * nh) & (lane_c < 4 * nh)
    lane_r = lax.broadcasted_iota(jnp.int32, (MLSTM_GATES, cl), 1)
    row_r = lax.broadcasted_iota(jnp.int32, (MLSTM_GATES, cl), 0)
    for c in range(tm // cl):
        rows = slice(c * cl, (c + 1) * cl)
        gc = g[rows]
        g1, g2, g3 = _split3(gc)
        prefix = _dot(tri, g1) + _dot(tri, g2) + _dot(tri, g3)
        suffix = prefix[cl - 1:cl, :] - prefix + gc
        prepared = jnp.where(fwd_forget, prefix, jnp.where(bwd_forget, suffix, gc))
        w_cols = pltpu.roll(prepared, nh, 1) - prepared
        grow_ref[c] = prepared.T[0:MLSTM_GATES, :]
        w_rows = w_cols.T[0:MLSTM_GATES, :]
        run_fwd, run_bwd = w_rows, w_rows
        shift = 1
        while shift < cl:
            run_fwd = jnp.maximum(run_fwd, jnp.where(lane_r >= shift, pltpu.roll(run_fwd, shift, 1), NEG_BIG))
            run_bwd = jnp.maximum(run_bwd, jnp.where(lane_r < cl - shift, pltpu.roll(run_bwd, cl - shift, 1), NEG_BIG))
            shift *= 2
        cm = jnp.where(row_r < 2 * nh, run_fwd, run_bwd)
        gcm_ref[c] = cm
        for direction in range(2):
            keep = (row >= col) if direction else (row <= col)
            last = 0 if direction else cl - 1
            for hd in range(nh):
                cf = 2 * nh * direction + nh + hd
                w_col = w_cols[:, cf:cf + 1]
                pexp_ref[direction, c, hd] = jnp.exp(jnp.where(keep, w_col - cm[cf:cf + 1, :], NEG_BIG)).astype(BF16)
                w_exp = jnp.exp(w_col - cm[cf:cf + 1, last:last + 1])
                kw_ref[direction, rows, hd * dk:(hd + 1) * dk] = (k32[rows, hd * dk:(hd + 1) * dk] * w_exp).astype(BF16)
    for part in range(4):
        project(part)


def _mlstm_in(x2, mod, gain, weights, seq):
    tokens = x2.shape[0]
    tm = MLSTM_IN_TILE
    per_row = seq // tm
    tile = lambda w: pl.BlockSpec((tm, w), lambda i: (i, 0))
    tile_t = lambda r: pl.BlockSpec((r, tm), lambda i: (0, i))
    qw, vw, nh, cl = MLSTM_QK_WIDTH, MLSTM_V_WIDTH, MLSTM_HEADS, SCAN_CHUNK
    cpt = tm // cl
    chunks = tokens // cl
    row_spec = pl.BlockSpec((cpt, MLSTM_GATES, cl), lambda i: (i, 0, 0))
    return pl.pallas_call(
        _mlstm_in_kernel,
        grid=(tokens // tm,),
        in_specs=[
            tile(D_MODEL),
            pl.BlockSpec((1, 6, D_MODEL), lambda i: (i // per_row, 0, 0)),
            _const_spec((1, D_MODEL)),
            _const_spec((qw, D_MODEL)),
            _const_spec((D_MODEL, qw)),
            _const_spec((vw, D_MODEL)),
            _const_spec((D_MODEL, vw)),
            _const_spec((D_MODEL, 2 * LANES)),
            _const_spec((1, LANES)),
        ],
        out_specs=[tile_t(qw), tile(qw), tile_t(nh * MLSTM_VT_ROWS), tile(vw),
                   pl.BlockSpec((2, cpt, nh, cl, cl), lambda i: (0, i, 0, 0, 0)),
                   pl.BlockSpec((2, tm, qw), lambda i: (0, i, 0)),
                   row_spec, row_spec],
        out_shape=[
            jax.ShapeDtypeStruct((qw, tokens), BF16),
            jax.ShapeDtypeStruct((tokens, qw), BF16),
            jax.ShapeDtypeStruct((nh * MLSTM_VT_ROWS, tokens), BF16),
            jax.ShapeDtypeStruct((tokens, vw), BF16),
            jax.ShapeDtypeStruct((2, chunks, nh, cl, cl), BF16),
            jax.ShapeDtypeStruct((2, tokens, qw), BF16),
            jax.ShapeDtypeStruct((chunks, MLSTM_GATES, cl), F32),
            jax.ShapeDtypeStruct((chunks, MLSTM_GATES, cl), F32),
        ],
        compiler_params=_params(("parallel",)),
        name="mlstm_in",
    )(x2, mod, gain, weights["wq_t"], weights["wk"], weights["wv_t"], weights["wo_gate"], weights["wg"],
      weights["b_gates"])


def _scan_block(qt_ref, k_ref, vt_ref, pexp_ref, kw_ref, grow_ref, gcm_ref, ct_ref, m_ref, reverse, emit):
    cl, dk, dv, vr = SCAN_CHUNK, MLSTM_QK_DIM, MLSTM_V_DIM, MLSTM_VT_ROWS
    nc = k_ref.shape[0] // cl
    direction = 1 if reverse else 0
    last = 0 if reverse else cl - 1
    items = [(h, c) for h in range(MLSTM_HEADS) for c in range(nc)]
    span = lambda c: slice(c * cl, (c + 1) * cl)
    qt = {(h, c): qt_ref[h * dk:(h + 1) * dk, span(c)] for h, c in items}
    vt = {(h, c): vt_ref[h * vr:(h + 1) * vr, span(c)] for h, c in items}

    p_t = {}
    for h, c in items:
        s_t = _dot(k_ref[span(c), h * dk:(h + 1) * dk], qt[h, c])
        p_t[h, c] = (s_t * pexp_ref[c, h].astype(F32)).astype(BF16)
    loc = {(h, c): _dot(vt[h, c], p_t[h, c]) for h, c in items}
    dct = {(h, c): _dot(vt[h, c], kw_ref[span(c), h * dk:(h + 1) * dk]) for h, c in items}

    heads = range(MLSTM_HEADS)
    state = {h: ct_ref[h] for h in heads}
    m_state = {h: m_ref[h][0:1, 0:1] for h in heads}
    for c in (reversed(range(nc)) if reverse else range(nc)):
        for h in heads:
            cf = 2 * MLSTM_HEADS * direction + MLSTM_HEADS + h
            b = grow_ref[c][cf:cf + 1, :]
            cm = gcm_ref[c][cf:cf + 1, :]
            m_inter = b + m_state[h]
            m_t = jnp.maximum(m_inter, b + cm)
            alpha = jnp.exp(b + cm - m_t)
            inter = jnp.exp(m_inter - m_t)
            carried = _dot(state[h].astype(BF16), qt[h, c])
            den = alpha * loc[h, c][dv:dv + 1, :] + inter * carried[dv:dv + 1, :]
            scale = 1.0 / jnp.maximum(jnp.abs(den), jnp.exp(-m_t))
            emit(h, c, (alpha * scale) * loc[h, c][0:dv, :] + (inter * scale) * carried[0:dv, :])
            b_last, g_max = b[:, last:last + 1], b[:, last:last + 1] + cm[:, last:last + 1]
            m_new = jnp.maximum(b_last + m_state[h], g_max)
            state[h] = jnp.exp(b_last + m_state[h] - m_new) * state[h] + jnp.exp(g_max - m_new) * dct[h, c]
            m_state[h] = m_new
    for h in heads:
        ct_ref[h] = state[h]
        m_ref[h] = jnp.broadcast_to(m_state[h], m_ref.shape[1:])


def _scan_init(ct_ref, m_ref):
    @pl.when(pl.program_id(1) == 0)
    def _():
        ct_ref[...] = jnp.zeros_like(ct_ref)
        m_ref[...] = jnp.zeros_like(m_ref)


def _scan_bwd_kernel(qt_ref, k_ref, vt_ref, pexp_ref, kw_ref, grow_ref, gcm_ref, ht_ref, ct_ref, m_ref):
    _scan_init(ct_ref, m_ref)
    cl, dv = SCAN_CHUNK, MLSTM_V_DIM

    def emit(h, c, val):
        ht_ref[h * dv:(h + 1) * dv, c * cl:(c + 1) * cl] = val

    _scan_block(qt_ref, k_ref, vt_ref, pexp_ref, kw_ref, grow_ref, gcm_ref, ct_ref, m_ref, True, emit)


def _scan_fwd_kernel(qt_ref, k_ref, vt_ref, pexp_ref, kw_ref, grow_ref, gcm_ref, hb_ref, og_ref, hn_ref,
                     y_ref, ct_ref, m_ref):
    _scan_init(ct_ref, m_ref)
    cl, dv = SCAN_CHUNK, MLSTM_V_DIM

    def emit(h, c, val):
        rows, cols = slice(c * cl, (c + 1) * cl), slice(h * dv, (h + 1) * dv)
        hs = val + hb_ref[cols, rows]
        hs = hs * lax.rsqrt(jnp.mean(hs * hs, axis=0, keepdims=True) + RMS_EPS) * hn_ref[cols, :]
        y_ref[rows, cols] = (og_ref[rows, cols].astype(F32) * hs.T).astype(BF16)

    _scan_block(qt_ref, k_ref, vt_ref, pexp_ref, kw_ref, grow_ref, gcm_ref, ct_ref, m_ref, False, emit)


def _mlstm_scans(qt, k, vt, og, pexp, kw, grow, gcm, head_norm_lanes, batch, seq):
    rows = SCAN_ROWS
    per_row = seq // rows
    qw, vw, nh, cl = MLSTM_QK_WIDTH, MLSTM_V_WIDTH, MLSTM_HEADS, SCAN_CHUNK
    tokens = batch * seq
    cpb = rows // cl

    def specs(reverse):
        direction = 1 if reverse else 0
        blk = lambda b, i: b * per_row + (per_row - 1 - i if reverse else i)
        nat = lambda width: pl.BlockSpec((rows, width), lambda b, i: (blk(b, i), 0))
        trn = lambda height: pl.BlockSpec((height, rows), lambda b, i: (0, blk(b, i)))
        gate = pl.BlockSpec((cpb, MLSTM_GATES, cl), lambda b, i: (blk(b, i), 0, 0))
        common = [trn(qw), nat(qw), trn(nh * MLSTM_VT_ROWS),
                  pl.BlockSpec((None, cpb, nh, cl, cl), lambda b, i: (direction, blk(b, i), 0, 0, 0)),
                  pl.BlockSpec((None, rows, qw), lambda b, i: (direction, blk(b, i), 0)),
                  gate, gate]
        return common, nat, trn

    scratch = [pltpu.VMEM((nh, MLSTM_VT_ROWS, MLSTM_QK_DIM), F32),
               pltpu.VMEM((nh, 8, LANES), F32)]
    params = _params(("parallel", "arbitrary"))
    common, nat, trn = specs(True)
    ht_bw = pl.pallas_call(
        _scan_bwd_kernel,
        grid=(batch, per_row),
        in_specs=common,
        out_specs=trn(vw),
        out_shape=jax.ShapeDtypeStruct((vw, tokens), F32),
        scratch_shapes=scratch,
        compiler_params=params,
        name="mlstm_bwd",
    )(qt, k, vt, pexp, kw, grow, gcm)
    common, nat, trn = specs(False)
    return pl.pallas_call(
        _scan_fwd_kernel,
        grid=(batch, per_row),
        in_specs=common + [trn(vw), nat(vw), _const_spec((vw, LANES))],
        out_specs=nat(vw),
        out_shape=jax.ShapeDtypeStruct((tokens, vw), BF16),
        scratch_shapes=scratch,
        compiler_params=params,
        name="mlstm_fwd",
    )(qt, k, vt, pexp, kw, grow, gcm, ht_bw, og, head_norm_lanes)


def _l1_post_kernel(x_ref, y_ref, mod_ref, g2_ref, gf_ref, wout_ref, wgu_ref, wd_ref, out_ref, act_ref):
    mod = mod_ref[0]
    n = x_ref.shape[0] // ROW_PARTS
    prepared = []
    for part in range(ROW_PARTS):
        rows = slice(part * n, (part + 1) * n)
        x1 = x_ref[rows, :] + mod[2:3] * _dot(y_ref[rows, :], wout_ref[...])
        prepared.append((rows, x1, _ffn_input(x1, mod, g2_ref[...])))
    for rows, x1, h in prepared:
        x2 = _ffn(x1, h, mod, wgu_ref, wd_ref, act_ref, rows)
        out_ref[rows, :] = x2 * lax.rsqrt(jnp.mean(x2 * x2, axis=-1, keepdims=True) + RMS_EPS) * gf_ref[...]


def _l1_post(x2, y, mod, g2, g_final, w_out, w_gu, w_down, seq):
    tokens = x2.shape[0]
    tm = TOKEN_TILE
    per_row = seq // tm
    tile = lambda w: pl.BlockSpec((tm, w), lambda i: (i, 0))
    return pl.pallas_call(
        _l1_post_kernel,
        grid=(tokens // tm,),
        in_specs=[
            tile(D_MODEL), tile(MLSTM_V_WIDTH),
            pl.BlockSpec((1, 6, D_MODEL), lambda i: (i // per_row, 0, 0)),
            _const_spec((1, D_MODEL)),
            _const_spec((1, D_MODEL)),
            _const_spec((MLSTM_V_WIDTH, D_MODEL)),
            _const_spec((D_MODEL, 2 * FFN_HIDDEN)),
            _const_spec((FFN_HIDDEN, D_MODEL)),
        ],
        out_specs=tile(D_MODEL),
        out_shape=jax.ShapeDtypeStruct((tokens, D_MODEL), F32),
        scratch_shapes=[pltpu.VMEM((tm, FFN_HIDDEN), BF16)],
        compiler_params=_params(("parallel",)),
        name="l1_post",
    )(x2, y, mod, g2, g_final, w_out, w_gu, w_down)


def _trunk(x, mod0, mod1, weights, tables):
    batch, seq, _ = x.shape
    assert seq % TOKEN_TILE == 0 and seq % MLSTM_IN_TILE == 0 and seq % SCAN_ROWS == 0
    x2 = x.reshape(batch * seq, D_MODEL)
    qkv = _qkv_proj(x2, mod0, weights["l0_norm1"], weights["w_qkv"], tables, batch, seq)
    outs, lses = zip(*[_attention_group(qkv[g], g) for g in range(len(ATT_GROUPS))])
    x2 = _l0_post(x2, outs, lses, mod0, weights["l0_norm2"], weights["w_o"], weights["l0_w_gu"],
                  weights["l0_w_down"], seq)
    scan_inputs = _mlstm_in(x2, mod1, weights["l1_norm1"], weights, seq)
    y = _mlstm_scans(*scan_inputs, weights["head_norm_lanes"], batch, seq)
    out = _l1_post(x2, y, mod1, weights["l1_norm2"], weights["final_norm"], weights["w_out"],
                   weights["l1_w_gu"], weights["l1_w_down"], seq)
    return out.reshape(batch, seq, D_MODEL)


def kernel(x_prompt, x_sample, c_prompt, c_sample, l0_ada_w, l0_ada_b, l0_norm1, l0_attn_w_qkv, l0_attn_w_o, l0_norm2, l0_ffn_w_gu, l0_ffn_w_down, l1_ada_w, l1_ada_b, l1_norm1, l1_mlstm_w_in, l1_mlstm_b_gates, l1_mlstm_head_norm, l1_mlstm_w_out, l1_norm2, l1_ffn_w_gu, l1_ffn_w_down, final_norm):
    row = lambda g: g.reshape(1, -1).astype(F32)
    qw, vw = MLSTM_QK_WIDTH, MLSTM_V_WIDTH
    w_gate =jnp.pad(l1_mlstm_w_in[:, MLSTM_MAIN_WIDTH:], ((0, 0), (0, LANES - MLSTM_GATES)))
    wg1 = w_gate.astype(BF16)
    weights = {
        "l0_norm1": row(l0_norm1), "l0_norm2": row(l0_norm2), "l1_norm1": row(l1_norm1), "l1_norm2": row(l1_norm2),
        "final_norm": row(final_norm),
        "head_norm_lanes": jnp.broadcast_to(l1_mlstm_head_norm.astype(F32)[:, None], (MLSTM_V_WIDTH, LANES)),
        "w_qkv": l0_attn_w_qkv.astype(BF16), "w_o": l0_attn_w_o.astype(BF16),
        "l0_w_gu": l0_ffn_w_gu.astype(BF16), "l0_w_down": l0_ffn_w_down.astype(BF16),
        "wq_t": l1_mlstm_w_in[:, :qw].T.astype(BF16), "wk": l1_mlstm_w_in[:, qw:2 * qw].astype(BF16),
        "wv_t": l1_mlstm_w_in[:, 2 * qw:2 * qw + vw].T.astype(BF16),
        "wo_gate": l1_mlstm_w_in[:, 2 * qw + vw:MLSTM_MAIN_WIDTH].astype(BF16),
        "wg": jnp.concatenate([wg1, (w_gate - wg1.astype(F32)).astype(BF16)], axis=1),
        "b_gates": jnp.pad(l1_mlstm_b_gates, (0, LANES - MLSTM_GATES)).reshape(1, LANES).astype(F32),
        "w_out": l1_mlstm_w_out.astype(BF16),
        "l1_w_gu": l1_ffn_w_gu.astype(BF16), "l1_w_down": l1_ffn_w_down.astype(BF16),
    }
    nb = x_prompt.shape[0]
    c_all = jnp.concatenate([c_prompt, c_sample], axis=0)
    mod0 = _ada_mod(c_all, l0_ada_w, l0_ada_b)
    mod1 = _ada_mod(c_all, l1_ada_w, l1_ada_b)
    tables = _rope_tables(max(x_prompt.shape[1], x_sample.shape[1]))
    y_prompt = _trunk(x_prompt, mod0[:nb], mod1[:nb], weights, tables)
    y_sample = _trunk(x_sample, mod0[nb:], mod1[nb:], weights, tables)
    return (y_prompt, y_sample)
```

```python
import functools
import math

import jax
import jax.numpy as jnp
from jax import lax
from jax.experimental import pallas as pl
from jax.experimental.pallas import tpu as pltpu

F32 = jnp.float32
BF16 = jnp.bfloat16

D_MODEL = 1024
ATT_GROUPS = ((128, 1), (512, 4), (2048, 16))
ATT_HEADS = 8
ATT_HEAD_DIM = 64
ATT_GROUP_WIDTH = ATT_HEADS * ATT_HEAD_DIM
ATT_QKV_WIDTH = len(ATT_GROUPS) * 3 * ATT_GROUP_WIDTH
ATT_RADIUS = ATT_GROUPS[0][0] // (2 * ATT_GROUPS[0][1])
assert all(w // (2 * d) == ATT_RADIUS for w, d in ATT_GROUPS)
ROPE_DIM = ATT_HEAD_DIM // 4
ROPE_HALF = ROPE_DIM // 2
ROPE_THETA = 500000.0
MLSTM_HEADS = 4
MLSTM_QK_DIM = D_MODEL // (2 * MLSTM_HEADS)
MLSTM_V_DIM = D_MODEL // MLSTM_HEADS
MLSTM_QK_WIDTH = MLSTM_HEADS * MLSTM_QK_DIM
MLSTM_V_WIDTH = MLSTM_HEADS * MLSTM_V_DIM
MLSTM_MAIN_WIDTH = 2 * MLSTM_QK_WIDTH + 2 * MLSTM_V_WIDTH
MLSTM_GATES = 4 * MLSTM_HEADS
MLSTM_VT_ROWS = MLSTM_V_DIM + 16
FFN_HIDDEN = int(math.ceil(8 * D_MODEL / 3 / 256)) * 256
RMS_EPS = 1e-6

LANES = 128
TOKEN_TILE = 512
MLSTM_IN_TILE = 1024
ATT_Q_BLOCK = 128
ATT_ROWS = 1024
ATT_STAGE_BLOCKS = 1
SCAN_CHUNK = 128
SCAN_ROWS = 1024
FFN_CHUNK = 256
NEG_BIG = -1e30
VMEM_LIMIT = 56 * 1024 * 1024


def _params(semantics):
    return pltpu.CompilerParams(dimension_semantics=semantics, vmem_limit_bytes=VMEM_LIMIT)


def _const_spec(shape):
    zeros = (0,) * len(shape)
    return pl.BlockSpec(shape, lambda *_: zeros, pipeline_mode=pl.Buffered(1))


def _split3(x):
    a = x.astype(BF16)
    r = x - a.astype(F32)
    b = r.astype(BF16)
    c = (r - b.astype(F32)).astype(BF16)
    return a, b, c


def _dot(a, b):
    return jnp.dot(a, b, preferred_element_type=F32)


def _norm_mod(x, gain, scale, shift):
    y = x * lax.rsqrt(jnp.mean(x * x, axis=-1, keepdims=True) + RMS_EPS)
    return (y * gain) * (1.0 + scale) + shift


def _ada_kernel(c_ref, w_ref, b_ref, o_ref):
    c = c_ref[...]
    a = c * jax.nn.sigmoid(c)
    a1, a2, _ = _split3(a)
    w1, w2, _ = _split3(w_ref[...])
    o_ref[...] = _dot(a1, w1) + _dot(a2, w1) + _dot(a1, w2) + b_ref[...]


def _ada_mod(c, ada_w, ada_b):
    nb = c.shape[0]
    width = ada_w.shape[1]
    tile = width // 4
    out = pl.pallas_call(
        _ada_kernel,
        grid=(4,),
        in_specs=[
            pl.BlockSpec((nb, D_MODEL), lambda j: (0, 0)),
            pl.BlockSpec((D_MODEL, tile), lambda j: (0, j)),
            pl.BlockSpec((1, tile), lambda j: (0, j)),
        ],
        out_specs=pl.BlockSpec((nb, tile), lambda j: (0, j)),
        out_shape=jax.ShapeDtypeStruct((nb, width), F32),
        compiler_params=_params(("arbitrary",)),
        name="ada_mod",
    )(c, ada_w, ada_b.reshape(1, width))
    return out.reshape(nb, 6, D_MODEL)


def _rope_tables(seq):
    inv = jnp.float32(ROPE_THETA) ** (-jnp.arange(ROPE_HALF, dtype=F32) / ROPE_HALF)
    ang = jnp.arange(seq, dtype=F32)[:, None] * inv[None, :]
    cos, sin = jnp.cos(ang), jnp.sin(ang)
    rest = ATT_HEAD_DIM - ROPE_DIM
    zeros = jnp.zeros((seq, ROPE_HALF), F32)
    c = jnp.concatenate([cos, cos, jnp.ones((seq, rest), F32)], axis=1)
    s_up = jnp.concatenate([-sin, zeros, jnp.zeros((seq, rest), F32)], axis=1)
    s_dn = jnp.concatenate([zeros, sin, jnp.zeros((seq, rest), F32)], axis=1)
    reps = LANES // ATT_HEAD_DIM
    return tuple(jnp.tile(t, (1, reps)) for t in (c, s_up, s_dn))


def _phase_major(table, dil, tile):
    if dil == 1:
        return table
    seq, width = table.shape
    return table.reshape(seq // tile, tile // dil, dil, width).transpose(0, 2, 1, 3).reshape(seq, width)


def _qkv_kernel(x_ref, mod_ref, g_ref, w_ref, *rest):
    ngrp = len(ATT_GROUPS)
    tab_refs, o_refs, stage_ref = rest[:3 * ngrp], rest[3 * ngrp:4 * ngrp], rest[4 * ngrp]
    mod = mod_ref[0]
    h32 = _norm_mod(x_ref[...], g_ref[...], mod[1:2], mod[0:1])
    gw = ATT_GROUP_WIDTH
    tm = x_ref.shape[0]
    nslab = D_MODEL // LANES
    for s in range(nslab):
        stage_ref[s] = h32[:, s * LANES:(s + 1) * LANES]
    for grp in range(ngrp):
        dil = ATT_GROUPS[grp][1]
        o_ref = o_refs[grp]
        cos, s_up, s_dn = (t[...] for t in tab_refs[3 * grp:3 * grp + 3])
        if dil == 1:
            h = h32.astype(BF16)
        else:
            h = jnp.concatenate(
                [jnp.concatenate([stage_ref[s, pl.ds(ph, tm // dil, stride=dil), :] for s in range(nslab)], axis=1)
                 for ph in range(dil)], axis=0).astype(BF16)
        for kind in range(3):
            blk = 3 * grp + kind
            p = _dot(h, w_ref[:, blk * gw:(blk + 1) * gw])
            for s in range(gw // LANES):
                val = p[:, s * LANES:(s + 1) * LANES]
                if kind < 2:
                    val = val * cos + pltpu.roll(val, LANES - ROPE_HALF, 1) * s_up + pltpu.roll(val, ROPE_HALF, 1) * s_dn
                if kind == 0:
                    val = val * (ATT_HEAD_DIM ** -0.5)
                val = val.astype(BF16)
                lo = kind * gw + s * LANES
                rows = tm // dil
                for ph in range(dil):
                    o_ref[0, :, ph * 3 * gw + lo:ph * 3 * gw + lo + LANES] = val[ph * rows:(ph + 1) * rows]


def _qkv_proj(x2, mod, gain, w_qkv, tables, batch, seq):
    tokens = x2.shape[0]
    tm = TOKEN_TILE
    per_row = seq // tm
    tab_spec = pl.BlockSpec((tm, LANES), lambda i: (i % per_row, 0))
    gw3 = 3 * ATT_GROUP_WIDTH
    dils = [d for _, d in ATT_GROUPS]
    group_tables = [_phase_major(t[:seq], d, tm) for d in dils for t in tables]
    return pl.pallas_call(
        _qkv_kernel,
        grid=(tokens // tm,),
        in_specs=[
            pl.BlockSpec((tm, D_MODEL), lambda i: (i, 0)),
            pl.BlockSpec((1, 6, D_MODEL), lambda i: (i // per_row, 0, 0)),
            _const_spec((1, D_MODEL)),
            _const_spec((D_MODEL, ATT_QKV_WIDTH)),
        ] + [tab_spec] * len(group_tables),
        out_specs=[pl.BlockSpec((1, tm // d, d * gw3), lambda i: (i // per_row, i % per_row, 0)) for d in dils],
        out_shape=[jax.ShapeDtypeStruct((batch, seq // d, d * gw3), BF16) for d in dils],
        scratch_shapes=[pltpu.VMEM((D_MODEL // LANES, tm, LANES), F32)],
        compiler_params=_params(("parallel",)),
        name="qkv_proj",
    )(x2, mod, gain, w_qkv, *group_tables)


def _attn_kernel(q_ref, k_ref, kp_ref, kn_ref, v_ref, vp_ref, vn_ref, o_ref, lse_ref, kbuf, vbuf, *, length, rows):
    rad, bq = ATT_RADIUS, ATT_Q_BLOCK
    nk = bq + 2 * rad
    i = pl.program_id(2)
    kbuf[0:rad, :] = kp_ref[0]
    kbuf[rad:rad + rows, :] = k_ref[0]
    kbuf[rad + rows:, :] = kn_ref[0]
    vbuf[0:rad, :] = vp_ref[0]
    vbuf[rad:rad + rows, :] = v_ref[0]
    vbuf[rad + rows:, :] = vn_ref[0]

    lane = lax.broadcasted_iota(jnp.int32, (bq, LANES), 1)
    first = lane < ATT_HEAD_DIM
    head_mask = (jnp.where(first, 1.0, 0.0).astype(BF16), jnp.where(first, 0.0, 1.0).astype(BF16))
    t_idx = lax.broadcasted_iota(jnp.int32, (bq, nk), 0)
    c_idx = lax.broadcasted_iota(jnp.int32, (bq, nk), 1)
    rel = c_idx - rad - t_idx
    cols = lambda hp: slice(hp * LANES, (hp + 1) * LANES)
    blocks = list(range(rows // bq))
    for batch in [blocks[j:j + ATT_STAGE_BLOCKS] for j in range(0, len(blocks), ATT_STAGE_BLOCKS)]:
        pairs = [(jb, hp) for jb in batch for hp in range(ATT_GROUP_WIDTH // LANES)]
        items = [(jb, hp, half) for jb, hp in pairs for half in range(2)]
        bias = {}
        for jb in batch:
            kpos = i * rows + jb * bq - rad + c_idx
            ok = (jnp.abs(rel) <= rad) & (kpos >= 0) & (kpos < length)
            bias[jb] = jnp.where(ok, 0.0, NEG_BIG)
        scores = {}
        for jb, hp, half in items:
            q2 = q_ref[0, jb * bq:(jb + 1) * bq, cols(hp)] * head_mask[half]
            scores[jb, hp, half] = _nt_dot(q2, kbuf[jb * bq:jb * bq + nk, cols(hp)]) + bias[jb]
        probs, den, lse = {}, {}, {}
        for it in items:
            m = jnp.max(scores[it], axis=1, keepdims=True)
            p = jnp.exp(scores[it] - m)
            den[it] = jnp.sum(p, axis=1, keepdims=True)
            lse[it] = m + jnp.log(den[it])
            probs[it] = p.astype(BF16)
        outs = {(jb, hp, half): _dot(probs[jb, hp, half], vbuf[jb * bq:jb * bq + nk, cols(hp)]) / den[jb, hp, half]
                for jb, hp, half in items}
        for jb, hp in pairs:
            o_ref[0, jb * bq:(jb + 1) * bq, cols(hp)] = jnp.where(first, outs[jb, hp, 0], outs[jb, hp, 1]).astype(BF16)
            lse_ref[0, jb * bq:(jb + 1) * bq, cols(hp)] = jnp.where(first, lse[jb, hp, 0], lse[jb, hp, 1])


def _attention_group(qkv, group):
    _, dil = ATT_GROUPS[group]
    batch, length, _ = qkv.shape
    rows, rad, gw = min(ATT_ROWS, length), ATT_RADIUS, ATT_GROUP_WIDTH
    assert length % rows == 0 and rows % ATT_Q_BLOCK == 0
    nblk = length // rows
    halo_per_blk = rows // rad
    last_halo = length // rad - 1
    view = qkv

    def main(kind):
        return pl.BlockSpec((1, rows, gw), lambda b, p, i: (b, i, 3 * p + kind))

    def prev(kind):
        return pl.BlockSpec((1, rad, gw), lambda b, p, i: (b, jnp.maximum(i * halo_per_blk - 1, 0), 3 * p + kind))

    def nxt(kind):
        return pl.BlockSpec((1, rad, gw), lambda b, p, i: (b, jnp.minimum((i + 1) * halo_per_blk, last_halo), 3 * p + kind))

    out_spec = pl.BlockSpec((1, rows, gw), lambda b, p, i: (b, i, p))
    return pl.pallas_call(
        functools.partial(_attn_kernel, length=length, rows=rows),
        grid=(batch, dil, nblk),
        in_specs=[main(0), main(1), prev(1), nxt(1), main(2), prev(2), nxt(2)],
        out_specs=[out_spec, out_spec],
        out_shape=[jax.ShapeDtypeStruct((batch, length, dil * gw), BF16),
                   jax.ShapeDtypeStruct((batch, length, dil * gw), F32)],
        scratch_shapes=[pltpu.VMEM((rows + 2 * rad, gw), BF16), pltpu.VMEM((rows + 2 * rad, gw), BF16)],
        compiler_params=_params(("parallel", "parallel", "parallel")),
        name=f"attn_g{group}",
    )(view, view, view, view, view, view, view)


ROW_PARTS = 2


def _ffn_input(x1, mod, g2):
    return _norm_mod(x1, g2, mod[4:5], mod[3:4]).astype(BF16)


def _ffn(x1, h, mod, wgu_ref, wd_ref, act_ref, rows):
    for lo in range(0, FFN_HIDDEN, FFN_CHUNK):
        hi = min(lo + FFN_CHUNK, FFN_HIDDEN)
        a = _dot(h, wgu_ref[:, lo:hi])
        b = _dot(h, wgu_ref[:, FFN_HIDDEN + lo:FFN_HIDDEN + hi])
        act_ref[rows, lo:hi] = (a * jax.nn.sigmoid(a) * b).astype(BF16)
    return x1 + mod[5:6] * _dot(act_ref[rows, :], wd_ref[...])


def _l0_post_kernel(x_ref, o0_ref, o1_ref, o2_ref, l0_ref, l1_ref, l2_ref, mod_ref, g2_ref,
                    wo_ref, wgu_ref, wd_ref, out_ref, act_ref, y_ref, stage_ref):
    mod = mod_ref[0]
    tm = x_ref.shape[0]
    gw = ATT_GROUP_WIDTH
    dils = [d for _, d in ATT_GROUPS]
    n = tm // ROW_PARTS
    prepared = []
    for part in range(ROW_PARTS):
        rows = slice(part * n, (part + 1) * n)
        slab = 0
        for s in range(gw // LANES):
            vals = []
            for ref, dil in zip((o0_ref, o1_ref, o2_ref, l0_ref, l1_ref, l2_ref), dils + dils):
                src_rows = slice(part * n // dil, (part + 1) * n // dil)
                if dil == 1:
                    vals.append(ref[0, src_rows, s * LANES:(s + 1) * LANES].astype(F32))
                    continue
                for ph in range(dil):
                    lo = ph * gw + s * LANES
                    stage_ref[slab, pl.ds(part * n + ph, n // dil, stride=dil), :] = (
                        ref[0, src_rows, lo:lo + LANES].astype(F32))
                vals.append(stage_ref[slab, rows, :])
                slab += 1
            o0, o1, o2, l0, l1, l2 = vals
            top = jnp.maximum(jnp.maximum(l0, l1), l2)
            e0, e1, e2 = jnp.exp(l0 - top), jnp.exp(l1 - top), jnp.exp(l2 - top)
            y_ref[rows, s * LANES:(s + 1) * LANES] = ((e0 * o0 + e1 * o1 + e2 * o2) / (e0 + e1 + e2)).astype(BF16)
        x1 = x_ref[rows, :] + mod[2:3] * _dot(y_ref[rows, :], wo_ref[...])
        prepared.append((rows, x1, _ffn_input(x1, mod, g2_ref[...])))
    for rows, x1, h in prepared:
        out_ref[rows, :] = _ffn(x1, h, mod, wgu_ref, wd_ref, act_ref, rows)


def _l0_post(x2, outs, lses, mod, g2, w_o, w_gu, w_down, seq):
    tokens = x2.shape[0]
    tm = TOKEN_TILE
    per_row = seq // tm
    gw = ATT_GROUP_WIDTH
    dils = [d for _, d in ATT_GROUPS]
    tile = lambda w: pl.BlockSpec((tm, w), lambda i: (i, 0))
    att_specs = [pl.BlockSpec((1, tm // d, d * gw), lambda i: (i // per_row, i % per_row, 0)) for d in dils]
    n_stage = 2 * (gw // LANES) * sum(1 for d in dils if d > 1)
    return pl.pallas_call(
        _l0_post_kernel,
        grid=(tokens // tm,),
        in_specs=[tile(D_MODEL)] + att_specs + att_specs + [
            pl.BlockSpec((1, 6, D_MODEL), lambda i: (i // per_row, 0, 0)),
            _const_spec((1, D_MODEL)),
            _const_spec((gw, D_MODEL)),
            _const_spec((D_MODEL, 2 * FFN_HIDDEN)),
            _const_spec((FFN_HIDDEN, D_MODEL)),
        ],
        out_specs=tile(D_MODEL),
        out_shape=jax.ShapeDtypeStruct((tokens, D_MODEL), F32),
        scratch_shapes=[pltpu.VMEM((tm, FFN_HIDDEN), BF16), pltpu.VMEM((tm, gw), BF16),
                        pltpu.VMEM((n_stage, tm, LANES), F32)],
        compiler_params=_params(("parallel",)),
        name="l0_post",
    )(x2, *outs, *lses, mod, g2, w_o, w_gu, w_down)


def _nt_dot(a, b):
    return lax.dot_general(a, b, (((1,), (1,)), ((), ())), preferred_element_type=F32)


def _mlstm_in_kernel(x_ref, mod_ref, g_ref, wqt_ref, wk_ref, wvt_ref, wo_ref, wg_ref, bg_ref,
                     qt_ref, k_ref, vt_ref, og_ref, pexp_ref, kw_ref, grow_ref, gcm_ref):
    mod = mod_ref[0]
    h32 = _norm_mod(x_ref[...], g_ref[...], mod[1:2], mod[0:1])
    h = h32.astype(BF16)
    dk, dv, vr = MLSTM_QK_DIM, MLSTM_V_DIM, MLSTM_VT_ROWS
    tm = x_ref.shape[0]
    ones_rows = jnp.where(lax.broadcasted_iota(jnp.int32, (vr - dv, tm), 0) == 0, 1.0, 0.0).astype(BF16)

    def project(part):
        if part == 0:
            qt_ref[...] = _nt_dot(wqt_ref[...], h).astype(BF16)
        elif part in (1, 2):
            for hd in range(2 * (part - 1), 2 * part):
                vt_ref[hd * vr:hd * vr + dv, :] = _nt_dot(wvt_ref[hd * dv:(hd + 1) * dv, :], h).astype(BF16)
                vt_ref[hd * vr + dv:(hd + 1) * vr, :] = ones_rows
        else:
            og_ref[...] = jax.nn.sigmoid(_dot(h, wo_ref[...])).astype(BF16)

    g2 = _dot(h, wg_ref[...])
    g = g2[:, 0:LANES] + g2[:, LANES:2 * LANES] + bg_ref[...]
    k32 = _dot(h, wk_ref[...]) * (dk ** -0.5)
    k_ref[...] = k32.astype(BF16)
    lane = lax.broadcasted_iota(jnp.int32, g.shape, 1)
    is_forget = (((lane >> 2) & 1) == 1) & (lane < MLSTM_GATES)
    log_sig = jnp.minimum(g, 0.0) - jnp.log(1.0 + jnp.exp(-jnp.abs(g)))
    g = jnp.where(is_forget, log_sig, g)
    cl, nh = SCAN_CHUNK, MLSTM_HEADS
    row = lax.broadcasted_iota(jnp.int32, (cl, cl), 0)
    col = lax.broadcasted_iota(jnp.int32, (cl, cl), 1)
    tri = jnp.where(col <= row, 1.0, 0.0).astype(BF16)
    lane_c = lax.broadcasted_iota(jnp.int32, (cl, LANES), 1)
    fwd_forget = (lane_c >= nh) & (lane_c < 2 * nh)
    bwd_forget = (lane_c >= 3 * nh) & (lane_c < 4 * nh)
    lane_r = lax.broadcasted_iota(jnp.int32, (MLSTM_GATES, cl), 1)
    row_r = lax.broadcasted_iota(jnp.int32, (MLSTM_GATES, cl), 0)
    for c in range(tm // cl):
        rows = slice(c * cl, (c + 1) * cl)
        gc = g[rows]
        g1, g2, g3 = _split3(gc)
        prefix = _dot(tri, g1) + _dot(tri, g2) + _dot(tri, g3)
        suffix = prefix[cl - 1:cl, :] - prefix + gc
        prepared = jnp.where(fwd_forget, prefix, jnp.where(bwd_forget, suffix, gc))
        w_cols = pltpu.roll(prepared, nh, 1) - prepared
        grow_ref[c] = prepared.T[0:MLSTM_GATES, :]
        w_rows = w_cols.T[0:MLSTM_GATES, :]
        run_fwd, run_bwd = w_rows, w_rows
        shift = 1
        while shift < cl:
            run_fwd = jnp.maximum(run_fwd, jnp.where(lane_r >= shift, pltpu.roll(run_fwd, shift, 1), NEG_BIG))
            run_bwd = jnp.maximum(run_bwd, jnp.where(lane_r < cl - shift, pltpu.roll(run_bwd, cl - shift, 1), NEG_BIG))
            shift *= 2
        cm = jnp.where(row_r < 2 * nh, run_fwd, run_bwd)
        gcm_ref[c] = cm
        for direction in range(2):
            keep = (row >= col) if direction else (row <= col)
            last = 0 if direction else cl - 1
            for hd in range(nh):
                cf = 2 * nh * direction + nh + hd
                w_col = w_cols[:, cf:cf + 1]
                pexp_ref[direction, c, hd] = jnp.exp(jnp.where(keep, w_col - cm[cf:cf + 1, :], NEG_BIG)).astype(BF16)
                w_exp = jnp.exp(w_col - cm[cf:cf + 1, last:last + 1])
                kw_ref[direction, rows, hd * dk:(hd + 1) * dk] = (k32[rows, hd * dk:(hd + 1) * dk] * w_exp).astype(BF16)
    for part in range(4):
        project(part)


def _mlstm_in(x2, mod, gain, weights, seq):
    tokens = x2.shape[0]
    tm = MLSTM_IN_TILE
    per_row = seq // tm
    tile = lambda w: pl.BlockSpec((tm, w), lambda i: (i, 0))
    tile_t = lambda r: pl.BlockSpec((r, tm), lambda i: (0, i))
    qw, vw, nh, cl = MLSTM_QK_WIDTH, MLSTM_V_WIDTH, MLSTM_HEADS, SCAN_CHUNK
    cpt = tm // cl
    chunks = tokens // cl
    row_spec = pl.BlockSpec((cpt, MLSTM_GATES, cl), lambda i: (i, 0, 0))
    return pl.pallas_call(
        _mlstm_in_kernel,
        grid=(tokens // tm,),
        in_specs=[
            tile(D_MODEL),
            pl.BlockSpec((1, 6, D_MODEL), lambda i: (i // per_row, 0, 0)),
            _const_spec((1, D_MODEL)),
            _const_spec((qw, D_MODEL)),
            _const_spec((D_MODEL, qw)),
            _const_spec((vw, D_MODEL)),
            _const_spec((D_MODEL, vw)),
            _const_spec((D_MODEL, 2 * LANES)),
            _const_spec((1, LANES)),
        ],
        out_specs=[tile_t(qw), tile(qw), tile_t(nh * MLSTM_VT_ROWS), tile(vw),
                   pl.BlockSpec((2, cpt, nh, cl, cl), lambda i: (0, i, 0, 0, 0)),
                   pl.BlockSpec((2, tm, qw), lambda i: (0, i, 0)),
                   row_spec, row_spec],
        out_shape=[
            jax.ShapeDtypeStruct((qw, tokens), BF16),
            jax.ShapeDtypeStruct((tokens, qw), BF16),
            jax.ShapeDtypeStruct((nh * MLSTM_VT_ROWS, tokens), BF16),
            jax.ShapeDtypeStruct((tokens, vw), BF16),
            jax.ShapeDtypeStruct((2, chunks, nh, cl, cl), BF16),
            jax.ShapeDtypeStruct((2, tokens, qw), BF16),
            jax.ShapeDtypeStruct((chunks, MLSTM_GATES, cl), F32),
            jax.ShapeDtypeStruct((chunks, MLSTM_GATES, cl), F32),
        ],
        compiler_params=_params(("parallel",)),
        name="mlstm_in",
    )(x2, mod, gain, weights["wq_t"], weights["wk"], weights["wv_t"], weights["wo_gate"], weights["wg"],
      weights["b_gates"])


def _scan_block(qt_ref, k_ref, vt_ref, pexp_ref, kw_ref, grow_ref, gcm_ref, ct_ref, m_ref, reverse, emit):
    cl, dk, dv, vr = SCAN_CHUNK, MLSTM_QK_DIM, MLSTM_V_DIM, MLSTM_VT_ROWS
    nc = k_ref.shape[0] // cl
    direction = 1 if reverse else 0
    last = 0 if reverse else cl - 1
    items = [(h, c) for h in range(MLSTM_HEADS) for c in range(nc)]
    span = lambda c: slice(c * cl, (c + 1) * cl)
    qt = {(h, c): qt_ref[h * dk:(h + 1) * dk, span(c)] for h, c in items}
    vt = {(h, c): vt_ref[h * vr:(h + 1) * vr, span(c)] for h, c in items}

    p_t = {}
    for h, c in items:
        s_t = _dot(k_ref[span(c), h * dk:(h + 1) * dk], qt[h, c])
        p_t[h, c] = (s_t * pexp_ref[c, h].astype(F32)).astype(BF16)
    loc = {(h, c): _dot(vt[h, c], p_t[h, c]) for h, c in items}
    dct = {(h, c): _dot(vt[h, c], kw_ref[span(c), h * dk:(h + 1) * dk]) for h, c in items}

    heads = range(MLSTM_HEADS)
    state = {h: ct_ref[h] for h in heads}
    m_state = {h: m_ref[h][0:1, 0:1] for h in heads}
    for c in (reversed(range(nc)) if reverse else range(nc)):
        for h in heads:
            cf = 2 * MLSTM_HEADS * direction + MLSTM_HEADS + h
            b = grow_ref[c][cf:cf + 1, :]
            cm = gcm_ref[c][cf:cf + 1, :]
            m_inter = b + m_state[h]
            m_t = jnp.maximum(m_inter, b + cm)
            alpha = jnp.exp(b + cm - m_t)
            inter = jnp.exp(m_inter - m_t)
            carried = _dot(state[h].astype(BF16), qt[h, c])
            den = alpha * loc[h, c][dv:dv + 1, :] + inter * carried[dv:dv + 1, :]
            scale = 1.0 / jnp.maximum(jnp.abs(den), jnp.exp(-m_t))
            emit(h, c, (alpha * scale) * loc[h, c][0:dv, :] + (inter * scale) * carried[0:dv, :])
            b_last, g_max = b[:, last:last + 1], b[:, last:last + 1] + cm[:, last:last + 1]
            m_new = jnp.maximum(b_last + m_state[h], g_max)
            state[h] = jnp.exp(b_last + m_state[h] - m_new) * state[h] + jnp.exp(g_max - m_new) * dct[h, c]
            m_state[h] = m_new
    for h in heads:
        ct_ref[h] = state[h]
        m_ref[h] = jnp.broadcast_to(m_state[h], m_ref.shape[1:])


def _scan_init(ct_ref, m_ref):
    @pl.when(pl.program_id(1) == 0)
    def _():
        ct_ref[...] = jnp.zeros_like(ct_ref)
        m_ref[...] = jnp.zeros_like(m_ref)


def _scan_bwd_kernel(qt_ref, k_ref, vt_ref, pexp_ref, kw_ref, grow_ref, gcm_ref, ht_ref, ct_ref, m_ref):
    _scan_init(ct_ref, m_ref)
    cl, dv = SCAN_CHUNK, MLSTM_V_DIM

    def emit(h, c, val):
        ht_ref[h * dv:(h + 1) * dv, c * cl:(c + 1) * cl] = val

    _scan_block(qt_ref, k_ref, vt_ref, pexp_ref, kw_ref, grow_ref, gcm_ref, ct_ref, m_ref, True, emit)


def _scan_fwd_kernel(qt_ref, k_ref, vt_ref, pexp_ref, kw_ref, grow_ref, gcm_ref, hb_ref, og_ref, hn_ref,
                     y_ref, ct_ref, m_ref):
    _scan_init(ct_ref, m_ref)
    cl, dv = SCAN_CHUNK, MLSTM_V_DIM

    def emit(h, c, val):
        rows, cols = slice(c * cl, (c + 1) * cl), slice(h * dv, (h + 1) * dv)
        hs = val + hb_ref[cols, rows]
        hs = hs * lax.rsqrt(jnp.mean(hs * hs, axis=0, keepdims=True) + RMS_EPS) * hn_ref[cols, :]
        y_ref[rows, cols] = (og_ref[rows, cols].astype(F32) * hs.T).astype(BF16)

    _scan_block(qt_ref, k_ref, vt_ref, pexp_ref, kw_ref, grow_ref, gcm_ref, ct_ref, m_ref, False, emit)


def _mlstm_scans(qt, k, vt, og, pexp, kw, grow, gcm, head_norm_lanes, batch, seq):
    rows = SCAN_ROWS
    per_row = seq // rows
    qw, vw, nh, cl = MLSTM_QK_WIDTH, MLSTM_V_WIDTH, MLSTM_HEADS, SCAN_CHUNK
    tokens = batch * seq
    cpb = rows // cl

    def specs(reverse):
        direction = 1 if reverse else 0
        blk = lambda b, i: b * per_row + (per_row - 1 - i if reverse else i)
        nat = lambda width: pl.BlockSpec((rows, width), lambda b, i: (blk(b, i), 0))
        trn = lambda height: pl.BlockSpec((height, rows), lambda b, i: (0, blk(b, i)))
        gate = pl.BlockSpec((cpb, MLSTM_GATES, cl), lambda b, i: (blk(b, i), 0, 0))
        common = [trn(qw), nat(qw), trn(nh * MLSTM_VT_ROWS),
                  pl.BlockSpec((None, cpb, nh, cl, cl), lambda b, i: (direction, blk(b, i), 0, 0, 0)),
                  pl.BlockSpec((None, rows, qw), lambda b, i: (direction, blk(b, i), 0)),
                  gate, gate]
        return common, nat, trn

    scratch = [pltpu.VMEM((nh, MLSTM_VT_ROWS, MLSTM_QK_DIM), F32),
               pltpu.VMEM((nh, 8, LANES), F32)]
    params = _params(("parallel", "arbitrary"))
    common, nat, trn = specs(True)
    ht_bw = pl.pallas_call(
        _scan_bwd_kernel,
        grid=(batch, per_row),
        in_specs=common,
        out_specs=trn(vw),
        out_shape=jax.ShapeDtypeStruct((vw, tokens), F32),
        scratch_shapes=scratch,
        compiler_params=params,
        name="mlstm_bwd",
    )(qt, k, vt, pexp, kw, grow, gcm)
    common, nat, trn = specs(False)
    return pl.pallas_call(
        _scan_fwd_kernel,
        grid=(batch, per_row),
        in_specs=common + [trn(vw), nat(vw), _const_spec((vw, LANES))],
        out_specs=nat(vw),
        out_shape=jax.ShapeDtypeStruct((tokens, vw), BF16),
        scratch_shapes=scratch,
        compiler_params=params,
        name="mlstm_fwd",
    )(qt, k, vt, pexp, kw, grow, gcm, ht_bw, og, head_norm_lanes)


def _l1_post_kernel(x_ref, y_ref, mod_ref, g2_ref, gf_ref, wout_ref, wgu_ref, wd_ref, out_ref, act_ref):
    mod = mod_ref[0]
    n = TOKEN_TILE // ROW_PARTS
    prepared = []
    for part in range(x_ref.shape[0] // n):
        rows = slice(part * n, (part + 1) * n)
        x1 = x_ref[rows, :] + mod[2:3] * _dot(y_ref[rows, :], wout_ref[...])
        prepared.append((rows, x1, _ffn_input(x1, mod, g2_ref[...])))
    for rows, x1, h in prepared:
        x2 = _ffn(x1, h, mod, wgu_ref, wd_ref, act_ref, rows)
        out_ref[rows, :] = x2 * lax.rsqrt(jnp.mean(x2 * x2, axis=-1, keepdims=True) + RMS_EPS) * gf_ref[...]


def _l1_post(x2, y, mod, g2, g_final, w_out, w_gu, w_down, seq):
    tokens = x2.shape[0]
    tm = 2 * TOKEN_TILE
    per_row = seq // tm
    tile = lambda w: pl.BlockSpec((tm, w), lambda i: (i, 0))
    return pl.pallas_call(
        _l1_post_kernel,
        grid=(tokens // tm,),
        in_specs=[
            tile(D_MODEL), tile(MLSTM_V_WIDTH),
            pl.BlockSpec((1, 6, D_MODEL), lambda i: (i // per_row, 0, 0)),
            _const_spec((1, D_MODEL)),
            _const_spec((1, D_MODEL)),
            _const_spec((MLSTM_V_WIDTH, D_MODEL)),
            _const_spec((D_MODEL, 2 * FFN_HIDDEN)),
            _const_spec((FFN_HIDDEN, D_MODEL)),
        ],
        out_specs=tile(D_MODEL),
        out_shape=jax.ShapeDtypeStruct((tokens, D_MODEL), F32),
        scratch_shapes=[pltpu.VMEM((tm, FFN_HIDDEN), BF16)],
        compiler_params=_params(("parallel",)),
        name="l1_post",
    )(x2, y, mod, g2, g_final, w_out, w_gu, w_down)


def _trunk(x, mod0, mod1, weights, tables):
    batch, seq, _ = x.shape
    assert seq % TOKEN_TILE == 0 and seq % MLSTM_IN_TILE == 0 and seq % SCAN_ROWS == 0
    x2 = x.reshape(batch * seq, D_MODEL)
    qkv = _qkv_proj(x2, mod0, weights["l0_norm1"], weights["w_qkv"], tables, batch, seq)
    outs, lses = zip(*[_attention_group(qkv[g], g) for g in range(len(ATT_GROUPS))])
    x2 = _l0_post(x2, outs, lses, mod0, weights["l0_norm2"], weights["w_o"], weights["l0_w_gu"],
                  weights["l0_w_down"], seq)
    scan_inputs = _mlstm_in(x2, mod1, weights["l1_norm1"], weights, seq)
    y = _mlstm_scans(*scan_inputs, weights["head_norm_lanes"], batch, seq)
    out = _l1_post(x2, y, mod1, weights["l1_norm2"], weights["final_norm"], weights["w_out"],
                   weights["l1_w_gu"], weights["l1_w_down"], seq)
    return out.reshape(batch, seq, D_MODEL)


def kernel(x_prompt, x_sample, c_prompt, c_sample, l0_ada_w, l0_ada_b, l0_norm1, l0_attn_w_qkv, l0_attn_w_o, l0_norm2, l0_ffn_w_gu, l0_ffn_w_down, l1_ada_w, l1_ada_b, l1_norm1, l1_mlstm_w_in, l1_mlstm_b_gates, l1_mlstm_head_norm, l1_mlstm_w_out, l1_norm2, l1_ffn_w_gu, l1_ffn_w_down, final_norm):
    row = lambda g: g.reshape(1, -1).astype(F32)
    qw, vw = MLSTM_QK_WIDTH, MLSTM_V_WIDTH
    w_gate =jnp.pad(l1_mlstm_w_in[:, MLSTM_MAIN_WIDTH:], ((0, 0), (0, LANES - MLSTM_GATES)))
    wg1 = w_gate.astype(BF16)
    weights = {
        "l0_norm1": row(l0_norm1), "l0_norm2": row(l0_norm2), "l1_norm1": row(l1_norm1), "l1_norm2": row(l1_norm2),
        "final_norm": row(final_norm),
        "head_norm_lanes": jnp.broadcast_to(l1_mlstm_head_norm.astype(F32)[:, None], (MLSTM_V_WIDTH, LANES)),
        "w_qkv": l0_attn_w_qkv.astype(BF16), "w_o": l0_attn_w_o.astype(BF16),
        "l0_w_gu": l0_ffn_w_gu.astype(BF16), "l0_w_down": l0_ffn_w_down.astype(BF16),
        "wq_t": l1_mlstm_w_in[:, :qw].T.astype(BF16), "wk": l1_mlstm_w_in[:, qw:2 * qw].astype(BF16),
        "wv_t": l1_mlstm_w_in[:, 2 * qw:2 * qw + vw].T.astype(BF16),
        "wo_gate": l1_mlstm_w_in[:, 2 * qw + vw:MLSTM_MAIN_WIDTH].astype(BF16),
        "wg": jnp.concatenate([wg1, (w_gate - wg1.astype(F32)).astype(BF16)], axis=1),
        "b_gates": jnp.pad(l1_mlstm_b_gates, (0, LANES - MLSTM_GATES)).reshape(1, LANES).astype(F32),
        "w_out": l1_mlstm_w_out.astype(BF16),
        "l1_w_gu": l1_ffn_w_gu.astype(BF16), "l1_w_down": l1_ffn_w_down.astype(BF16),
    }
    nb = x_prompt.shape[0]
    c_all = jnp.concatenate([c_prompt, c_sample], axis=0)
    mod0 = _ada_mod(c_all, l0_ada_w, l0_ada_b)
    mod1 = _ada_mod(c_all, l1_ada_w, l1_ada_b)
    tables = _rope_tables(max(x_prompt.shape[1], x_sample.shape[1]))
    y_prompt = _trunk(x_prompt, mod0[:nb], mod1[:nb], weights, tables)
    y_sample = _trunk(x_sample, mod0[nb:], mod1[nb:], weights, tables)
    return (y_prompt, y_sample)
```
